```python
import math
import jax
import jax.numpy as jnp
from jax import lax
import numpy as np

D_MODEL = 4096
BATCH = 1
SEQ = 16384
DEPTH = 4
DEC_BATCH = 4
DEC_SEQ = 4096
PAST_LEN = 128

N_MIXERS = 2
EXPAND = 2
D_INNER = EXPAND * D_MODEL
S5_GROUP = 16
S5_GROUPS = D_INNER // S5_GROUP
S5_STATE = 64
MLSTM_HEADS = 16
MLSTM_HEAD_DIM = D_INNER // MLSTM_HEADS
QKV_BLOCK = 4
CONV_K = 5
CHUNK = 128
MEM_TOKENS = 256
XA_HEADS = 4
XA_HEAD_DIM = D_MODEL // XA_HEADS
N_A_LAYERS = (DEPTH + 1) // 2
N_B_LAYERS = DEPTH // 2
EPS = 1e-6
STEP_MIN = 1e-3
STEP_MAX = 1e-1

kernel_name = "hybrid_bidir_s5_mlstm_memxattn"


def rmsnorm(x, g):
    xf = x.astype(jnp.float32)
    y = xf * lax.rsqrt(jnp.mean(xf * xf, axis=-1, keepdims=True) + EPS)
    return (y * g.astype(jnp.float32)).astype(x.dtype)


def to_chunks(t):
    b, l = t.shape[:2]
    t = t.reshape((b, l // CHUNK, CHUNK) + t.shape[2:])
    return jnp.moveaxis(t, 1, 0)


def from_chunks(t):
    t = jnp.moveaxis(t, 0, 1)
    return t.reshape((t.shape[0], t.shape[1] * t.shape[2]) + t.shape[3:])


def _complex_affine_combine(e1, e2):
    a1r, a1i, b1r, b1i = e1
    a2r, a2i, b2r, b2i = e2
    ar = a1r * a2r - a1i * a2i
    ai = a1r * a2i + a1i * a2r
    br = a2r * b1r - a2i * b1i + b2r
    bi = a2r * b1i + a2i * b1r + b2i
    return ar, ai, br, bi


def s5_scan(u, a_re, a_im, log_step, b_re, b_im, c_re, c_im):
    f32 = jnp.float32
    lam_re = jnp.minimum(a_re.astype(f32), -1e-4)
    lam_im = a_im.astype(f32)
    dt = jnp.exp(log_step.astype(f32))[:, None]
    mag = jnp.exp(lam_re * dt)
    ab_re = mag * jnp.cos(lam_im * dt)
    ab_im = mag * jnp.sin(lam_im * dt)
    den = lam_re * lam_re + lam_im * lam_im
    nr = ab_re - 1.0
    ni = ab_im
    f_re = (nr * lam_re + ni * lam_im) / den
    f_im = (ni * lam_re - nr * lam_im) / den
    b_re = b_re.astype(f32)
    b_im = b_im.astype(f32)
    bb_re = f_re[..., None] * b_re - f_im[..., None] * b_im
    bb_im = f_re[..., None] * b_im + f_im[..., None] * b_re
    c_re = c_re.astype(f32)
    c_im = c_im.astype(f32)
    bsz = u.shape[0]

    def step(carry, uc):
        xr, xi = carry
        bur = jnp.einsum('btgp,gnp->btgn', uc, bb_re)
        bui = jnp.einsum('btgp,gnp->btgn', uc, bb_im)
        ar = jnp.broadcast_to(ab_re, bur.shape)
        ai = jnp.broadcast_to(ab_im, bur.shape)
        pr, pim, sr, si = lax.associative_scan(_complex_affine_combine, (ar, ai, bur, bui), axis=1)
        xr_t = sr + pr * xr[:, None] - pim * xi[:, None]
        xi_t = si + pr * xi[:, None] + pim * xr[:, None]
        y = jnp.einsum('btgn,gpn->btgp', xr_t, c_re) - jnp.einsum('btgn,gpn->btgp', xi_t, c_im)
        return (xr_t[:, -1], xi_t[:, -1]), y

    zeros = jnp.zeros((bsz, S5_GROUPS, S5_STATE), f32)
    _, ys = lax.scan(step, (zeros, zeros), to_chunks(u))
    return from_chunks(ys)


def s5_mixer(h, in_w, a_re, a_im, log_step, b_re, b_im, c_re, c_im, d_skip, glu_w, glu_b, out_w):
    uz = h @ in_w
    u, z = jnp.split(uz, 2, axis=-1)
    bsz, l, _ = u.shape
    uf = u.astype(jnp.float32)
    ug = uf.reshape(bsz, l, S5_GROUPS, S5_GROUP)
    y_fwd = s5_scan(ug, a_re[0], a_im[0], log_step[0], b_re[0], b_im[0], c_re[0], c_im[0])
    y_bwd = jnp.flip(s5_scan(jnp.flip(ug, 1), a_re[1], a_im[1], log_step[1], b_re[1], b_im[1], c_re[1], c_im[1]), 1)
    y = (y_fwd + y_bwd).reshape(bsz, l, D_INNER) + d_skip.astype(jnp.float32) * uf
    y = jax.nn.gelu(y.astype(h.dtype))
    y = y * jax.nn.sigmoid(y @ glu_w + glu_b)
    return (y * jax.nn.silu(z)) @ out_w


def mlstm_chunked(q, k, v, log_i, log_f):
    bsz, _, nh, dk = q.shape
    dv = v.shape[-1]
    qc = jnp.swapaxes(to_chunks(q), 2, 3)
    kc = jnp.swapaxes(to_chunks(k), 2, 3)
    vc = jnp.swapaxes(to_chunks(v), 2, 3)
    ic = jnp.swapaxes(to_chunks(log_i), 2, 3)
    fc = jnp.swapaxes(to_chunks(log_f), 2, 3)
    mask = jnp.tril(jnp.ones((CHUNK, CHUNK), dtype=bool))

    def step(carry, inp):
        c_mat, n_vec, m = carry
        qb, kb, vb, ib, fb = inp
        bcum = jnp.cumsum(fb, axis=-1)
        dmat = bcum[..., :, None] - bcum[..., None, :] + ib[..., None, :]
        dmat = jnp.where(mask, dmat, -jnp.inf)
        inter = bcum + m[..., None]
        m_t = jnp.maximum(inter, jnp.max(dmat, axis=-1))
        w_inter = jnp.exp(inter - m_t)
        s = jnp.einsum('bhtd,bhsd->bhts', qb, kb) * jnp.exp(dmat - m_t[..., None])
        num = w_inter[..., None] * jnp.einsum('bhtd,bhde->bhte', qb, c_mat) + jnp.einsum('bhts,bhse->bhte', s, vb)
        den = w_inter * jnp.einsum('bhtd,bhd->bht', qb, n_vec) + jnp.sum(s, axis=-1)
        h = num / jnp.maximum(jnp.abs(den), jnp.exp(-m_t))[..., None]
        b_last = bcum[..., -1]
        g = b_last[..., None] - bcum + ib
        m_new = jnp.maximum(b_last + m, jnp.max(g, axis=-1))
        decay = jnp.exp(b_last + m - m_new)
        wg = jnp.exp(g - m_new[..., None])
        c_new = decay[..., None, None] * c_mat + jnp.einsum('bhs,bhsd,bhse->bhde', wg, kb, vb)
        n_new = decay[..., None] * n_vec + jnp.einsum('bhs,bhsd->bhd', wg, kb)
        return (c_new, n_new, m_new), h

    f32 = jnp.float32
    init = (jnp.zeros((bsz, nh, dk, dv), f32), jnp.zeros((bsz, nh, dk), f32), jnp.zeros((bsz, nh), f32))
    _, hs = lax.scan(step, init, (qc, kc, vc, ic, fc))
    return from_chunks(jnp.swapaxes(hs, 2, 3))


def block_diag_proj(x, w):
    bsz, l, e = x.shape
    xb = x.reshape(bsz, l, e // QKV_BLOCK, QKV_BLOCK)
    return jnp.einsum('blnc,ncd->blnd', xb, w).reshape(bsz, l, e)


def mlstm_mixer(h, in_w, gate_b, conv_w, conv_b, wq, wk, wv, skip, head_norm, out_w):
    bsz, l, _ = h.shape
    f32 = jnp.float32
    proj = h @ in_w
    xm, z, o_pre, gates = jnp.split(proj, [D_INNER, 2 * D_INNER, 3 * D_INNER], axis=-1)
    gates = (gates + gate_b).astype(f32)
    xconv = lax.conv_general_dilated(xm, conv_w[:, None, :], window_strides=(1,),
                                     padding=[(CONV_K // 2, CONV_K // 2)],
                                     dimension_numbers=('NWC', 'WIO', 'NWC'),
                                     feature_group_count=D_INNER)
    xc = jax.nn.silu(xconv + conv_b)
    shp = (bsz, l, MLSTM_HEADS, MLSTM_HEAD_DIM)
    q = block_diag_proj(xc, wq).astype(f32).reshape(shp)
    k = block_diag_proj(xc, wk).astype(f32).reshape(shp) * (MLSTM_HEAD_DIM ** -0.5)
    v = block_diag_proj(xm, wv).astype(f32).reshape(shp)
    i_f, f_f, i_b, f_b = jnp.split(gates, 4, axis=-1)
    h_fwd = mlstm_chunked(q, k, v, i_f, jax.nn.log_sigmoid(f_f))
    h_bwd = jnp.flip(mlstm_chunked(jnp.flip(q, 1), jnp.flip(k, 1), jnp.flip(v, 1),
                                   jnp.flip(i_b, 1), jnp.flip(jax.nn.log_sigmoid(f_b), 1)), 1)
    hs = h_fwd + h_bwd
    hs = hs * lax.rsqrt(jnp.mean(hs * hs, axis=-1, keepdims=True) + EPS)
    hs = hs.reshape(bsz, l, D_INNER).astype(h.dtype) * head_norm
    hs = jax.nn.sigmoid(o_pre) * hs + skip * xc
    return (hs * jax.nn.silu(z)) @ out_w


def cross_attn(h, mem_n, wq, wk, wv, wo):
    bsz, l, _ = h.shape
    m = mem_n.shape[1]
    q = (h @ wq).reshape(bsz, l, XA_HEADS, XA_HEAD_DIM)
    k = (mem_n @ wk).reshape(bsz, m, XA_HEADS, XA_HEAD_DIM)
    v = (mem_n @ wv).reshape(bsz, m, XA_HEADS, XA_HEAD_DIM)
    s = jnp.einsum('blhd,bmhd->bhlm', q, k).astype(jnp.float32) * (XA_HEAD_DIM ** -0.5)
    p = jax.nn.softmax(s, axis=-1).astype(v.dtype)
    o = jnp.einsum('bhlm,bmhd->blhd', p, v).reshape(bsz, l, D_MODEL)
    return o @ wo


def trunk(x, mem, mix_norm, a_in_w, s5_a_re, s5_a_im, s5_log_step, s5_b_re, s5_b_im, s5_c_re, s5_c_im,
          s5_d, a_glu_w, a_glu_b, a_out_w, b_in_w, b_gate_bias, b_conv_w, b_conv_b, b_wq, b_wk, b_wv,
          b_skip, b_head_norm, b_out_w, xa_norm, mem_norm, xa_wq, xa_wk, xa_wv, xa_wo, final_norm):
    for i in range(DEPTH):
        j = i // N_MIXERS
        h = rmsnorm(x, mix_norm[i])
        if i % N_MIXERS == 0:
            x = x + s5_mixer(h, a_in_w[j], s5_a_re[j], s5_a_im[j], s5_log_step[j], s5_b_re[j], s5_b_im[j],
                             s5_c_re[j], s5_c_im[j], s5_d[j], a_glu_w[j], a_glu_b[j], a_out_w[j])
        else:
            x = x + mlstm_mixer(h, b_in_w[j], b_gate_bias[j], b_conv_w[j], b_conv_b[j], b_wq[j], b_wk[j],
                                b_wv[j], b_skip[j], b_head_norm[j], b_out_w[j])
        mem_n = rmsnorm(mem, mem_norm[i])
        x = x + cross_attn(rmsnorm(x, xa_norm[i]), mem_n, xa_wq[i], xa_wk[i], xa_wv[i], xa_wo[i])
    return rmsnorm(x, final_norm)


def setup_inputs(seed: int = 0) -> dict:
    key = jax.random.key(seed)
    ks = iter(jax.random.split(key, 48))
    f32 = jnp.float32

    def nrm(shape, std):
        return jax.random.normal(next(ks), shape, f32) * std

    D, E, G, N, P, H = D_MODEL, D_INNER, S5_GROUPS, S5_STATE, S5_GROUP, MLSTM_HEADS
    NA, NB = N_A_LAYERS, N_B_LAYERS
    x_prompt = nrm((BATCH, SEQ, D), 1.0)
    x_sample = nrm((DEC_BATCH, DEC_SEQ, D), 1.0)
    mem_prompt = nrm((BATCH, MEM_TOKENS, D), 1.0)
    mem_sample = nrm((DEC_BATCH, MEM_TOKENS, D), 1.0)
    mix_norm = 1.0 + nrm((DEPTH, D), 0.02)
    a_in_w = nrm((NA, D, 2 * E), D ** -0.5)
    s5_a_re = -0.5 + nrm((NA, 2, G, N), 0.01)
    s5_a_im = jnp.pi * jnp.arange(N, dtype=f32) + nrm((NA, 2, G, N), 0.01)
    s5_log_step = jax.random.uniform(next(ks), (NA, 2, G), f32, math.log(STEP_MIN), math.log(STEP_MAX))
    s5_b_re = nrm((NA, 2, G, N, P), (2.0 * P) ** -0.5)
    s5_b_im = nrm((NA, 2, G, N, P), (2.0 * P) ** -0.5)
    s5_c_re = nrm((NA, 2, G, P, N), N ** -0.5)
    s5_c_im = nrm((NA, 2, G, P, N), N ** -0.5)
    s5_d = nrm((NA, E), 1.0)
    a_glu_w = nrm((NA, E, E), E ** -0.5)
    a_glu_b = nrm((NA, E), 0.02)
    a_out_w = nrm((NA, E, D), E ** -0.5)
    b_in_w = nrm((NB, D, 3 * E + 4 * H), D ** -0.5)
    f_bias = jnp.linspace(3.0, 6.0, H, dtype=f32)
    b_gate_bias = jnp.concatenate([nrm((NB, H), 0.1), f_bias + nrm((NB, H), 0.1),
                                   nrm((NB, H), 0.1), f_bias + nrm((NB, H), 0.1)], axis=-1)
    b_conv_w = nrm((NB, CONV_K, E), CONV_K ** -0.5)
    b_conv_b = nrm((NB, E), 0.02)
    b_wq = nrm((NB, E // QKV_BLOCK, QKV_BLOCK, QKV_BLOCK), QKV_BLOCK ** -0.5)
    b_wk = nrm((NB, E // QKV_BLOCK, QKV_BLOCK, QKV_BLOCK), QKV_BLOCK ** -0.5)
    b_wv = nrm((NB, E // QKV_BLOCK, QKV_BLOCK, QKV_BLOCK), QKV_BLOCK ** -0.5)
    b_skip = 1.0 + nrm((NB, E), 0.1)
    b_head_norm = 1.0 + nrm((NB, E), 0.02)
    b_out_w = nrm((NB, E, D), E ** -0.5)
    xa_norm = 1.0 + nrm((DEPTH, D), 0.02)
    mem_norm = 1.0 + nrm((DEPTH, D), 0.02)
    xa_wq = nrm((DEPTH, D, D), D ** -0.5)
    xa_wk = nrm((DEPTH, D, D), D ** -0.5)
    xa_wv = nrm((DEPTH, D, D), D ** -0.5)
    xa_wo = nrm((DEPTH, D, D), D ** -0.5)
    final_norm = 1.0 + nrm((D,), 0.02)
    return {"x_prompt": x_prompt, "x_sample": x_sample, "mem_prompt": mem_prompt, "mem_sample": mem_sample,
            "mix_norm": mix_norm, "a_in_w": a_in_w, "s5_a_re": s5_a_re, "s5_a_im": s5_a_im,
            "s5_log_step": s5_log_step, "s5_b_re": s5_b_re, "s5_b_im": s5_b_im, "s5_c_re": s5_c_re,
            "s5_c_im": s5_c_im, "s5_d": s5_d, "a_glu_w": a_glu_w, "a_glu_b": a_glu_b, "a_out_w": a_out_w,
            "b_in_w": b_in_w, "b_gate_bias": b_gate_bias, "b_conv_w": b_conv_w, "b_conv_b": b_conv_b,
            "b_wq": b_wq, "b_wk": b_wk, "b_wv": b_wv, "b_skip": b_skip, "b_head_norm": b_head_norm,
            "b_out_w": b_out_w, "xa_norm": xa_norm, "mem_norm": mem_norm, "xa_wq": xa_wq, "xa_wk": xa_wk,
            "xa_wv": xa_wv, "xa_wo": xa_wo, "final_norm": final_norm}


def reference(x_prompt, x_sample, mem_prompt, mem_sample, mix_norm, a_in_w, s5_a_re, s5_a_im, s5_log_step,
              s5_b_re, s5_b_im, s5_c_re, s5_c_im, s5_d, a_glu_w, a_glu_b, a_out_w, b_in_w, b_gate_bias,
              b_conv_w, b_conv_b, b_wq, b_wk, b_wv, b_skip, b_head_norm, b_out_w, xa_norm, mem_norm,
              xa_wq, xa_wk, xa_wv, xa_wo, final_norm):
    weights = (mix_norm, a_in_w, s5_a_re, s5_a_im, s5_log_step, s5_b_re, s5_b_im, s5_c_re, s5_c_im, s5_d,
               a_glu_w, a_glu_b, a_out_w, b_in_w, b_gate_bias, b_conv_w, b_conv_b, b_wq, b_wk, b_wv,
               b_skip, b_head_norm, b_out_w, xa_norm, mem_norm, xa_wq, xa_wk, xa_wv, xa_wo, final_norm)
    y_prompt = trunk(x_prompt, mem_prompt, *weights)
    y_sample = trunk(x_sample, mem_sample, *weights)
    return (y_prompt, y_sample)
```

```python
import functools
import math

import jax
import jax.numpy as jnp
from jax import lax
from jax.experimental import pallas as pl
from jax.experimental.pallas import tpu as pltpu

F32 = jnp.float32
BF16 = jnp.bfloat16

XA_HEADS = 4
EPS = 1e-6
S5_TB = 16
S5_PAIR_BLOCK = 2
S5_ROW_CHUNK = 256
MLSTM_CHUNK = 256
MLSTM_HEAD_BLOCK = 8
QKV_DENSE = 256
LANES = 128
SUBLANES = 8
V7X_VMEM_LIMIT = 56 * 1024 * 1024
NEG_BIG = -1e30


def _tile(n, pref, mult):
    best = None
    for t in range(mult, min(n, pref) + 1, mult):
        if n % t == 0:
            best = t
    return n if best is None else best


def _params(sem):
    return pltpu.CompilerParams(dimension_semantics=sem, vmem_limit_bytes=V7X_VMEM_LIMIT)


def _rmsnorm_kernel(x_ref, g_ref, o_ref):
    x = x_ref[...].astype(F32)
    ms = jnp.mean(x * x, axis=-1, keepdims=True)
    o_ref[...] = (x * lax.rsqrt(ms + EPS) * g_ref[...]).astype(o_ref.dtype)


def rmsnorm(x, g, out_dtype):
    t, d = x.shape
    tm = _tile(t, 512, 8)
    return pl.pallas_call(
        _rmsnorm_kernel,
        name="rmsnorm",
        grid=(t // tm,),
        in_specs=[pl.BlockSpec((tm, d), lambda i: (i, 0)),
                  pl.BlockSpec((1, d), lambda i: (0, 0))],
        out_specs=pl.BlockSpec((tm, d), lambda i: (i, 0)),
        out_shape=jax.ShapeDtypeStruct((t, d), out_dtype),
        compiler_params=_params(("parallel",)),
    )(x, g.reshape(1, d).astype(F32))


def _sigmoid(x):
    return 1.0 / (1.0 + jnp.exp(-x))


def _silu(x):
    return x * _sigmoid(x)


def _mm_epilogue(mode, acc, extra):
    if mode == "plain":
        return acc
    if mode == "bias":
        return acc + extra[0][...]
    if mode == "scale":
        return acc * extra[0][...]
    if mode == "residual":
        return extra[0][...] + acc
    if mode == "glu":
        b_ref, y_ref, z_ref = extra
        y = y_ref[...].astype(F32)
        return y * _sigmoid(acc + b_ref[...]) * _silu(z_ref[...])
    raise ValueError(mode)


def _mm_kernel(*refs, nk, mode, n_extra):
    a_ref, w_ref = refs[0], refs[1]
    extra = refs[2:2 + n_extra]
    o_ref = refs[2 + n_extra]
    part = jnp.dot(a_ref[...], w_ref[...], preferred_element_type=F32)
    if nk == 1:
        o_ref[...] = _mm_epilogue(mode, part, extra).astype(o_ref.dtype)
        return
    acc_ref = refs[3 + n_extra]
    k = pl.program_id(2)

    @pl.when(k == 0)
    def _():
        acc_ref[...] = part

    @pl.when(jnp.logical_and(k > 0, k < nk - 1))
    def _():
        acc_ref[...] += part

    @pl.when(k == nk - 1)
    def _():
        o_ref[...] = _mm_epilogue(mode, acc_ref[...] + part, extra).astype(o_ref.dtype)


def matmul(a, w, *, out_dtype, mode="plain", extra=(), extra_specs=(), tm=1024, tn=1024, tk=4096):
    m, kd = a.shape
    n = w.shape[1]
    tm, tn, tk = _tile(m, tm, 8), _tile(n, tn, LANES), _tile(kd, tk, LANES)
    nk = kd // tk
    kern = functools.partial(_mm_kernel, nk=nk, mode=mode, n_extra=len(extra))
    in_specs = [pl.BlockSpec((tm, tk), lambda i, j, k: (i, k)),
                pl.BlockSpec((tk, tn), lambda i, j, k: (k, j))]
    in_specs += [mk(tm, tn) for mk in extra_specs]
    return pl.pallas_call(
        kern,
        name="matmul_" + mode,
        grid=(m // tm, n // tn, nk),
        in_specs=in_specs,
        out_specs=pl.BlockSpec((tm, tn), lambda i, j, k: (i, j)),
        out_shape=jax.ShapeDtypeStruct((m, n), out_dtype),
        scratch_shapes=[pltpu.VMEM((tm, tn), F32)] if nk > 1 else [],
        compiler_params=_params(("parallel", "parallel", "arbitrary")),
    )(a, w, *extra)


def _row_spec(tm, tn):
    return pl.BlockSpec((1, tn), lambda i, j, k: (0, j))


def _tile_spec(col_off=0):
    def mk(tm, tn):
        off = col_off // tn
        return pl.BlockSpec((tm, tn), lambda i, j, k: (i, j + off))
    return mk


def _s5_operators(a_re, a_im, log_step, b_re, b_im, c_re, c_im, d_skip):
    hi = lax.Precision.HIGHEST
    tb = S5_TB
    _, g, n = a_re.shape
    p = b_re.shape[-1]
    lam_re = jnp.minimum(a_re.astype(F32), -1e-4)
    lam_im = a_im.astype(F32)
    dt = jnp.exp(log_step.astype(F32))[..., None]
    mag = jnp.exp(lam_re * dt)
    ab_re = mag * jnp.cos(lam_im * dt)
    ab_im = mag * jnp.sin(lam_im * dt)
    den = lam_re * lam_re + lam_im * lam_im
    nr, ni = ab_re - 1.0, ab_im
    f_re = (nr * lam_re + ni * lam_im) / den
    f_im = (ni * lam_re - nr * lam_im) / den
    b_re, b_im = b_re.astype(F32), b_im.astype(F32)
    bb_re = f_re[..., None] * b_re - f_im[..., None] * b_im
    bb_im = f_re[..., None] * b_im + f_im[..., None] * b_re
    c_re, c_im = c_re.astype(F32), c_im.astype(F32)

    ks = jnp.arange(tb + 1, dtype=F32)[:, None, None, None]
    pmag = jnp.exp(ks * (lam_re * dt)[None])
    pang = ks * (lam_im * dt)[None]
    pw_re, pw_im = pmag * jnp.cos(pang), pmag * jnp.sin(pang)

    e_re = c_re[None] * pw_re[:, :, :, None, :] - c_im[None] * pw_im[:, :, :, None, :]
    e_im = c_re[None] * pw_im[:, :, :, None, :] + c_im[None] * pw_re[:, :, :, None, :]
    kern = (jnp.einsum("kdgpn,dgnq->kdgpq", e_re[:tb], bb_re, precision=hi)
            - jnp.einsum("kdgpn,dgnq->kdgpq", e_im[:tb], bb_im, precision=hi))

    t_idx = jnp.arange(tb)
    lag = t_idx[None, :] - t_idx[:, None]
    kf = jnp.where((lag >= 0)[:, :, None, None, None], kern[jnp.clip(lag, 0, tb - 1), 0], 0.0)
    kb = jnp.where((lag <= 0)[:, :, None, None, None], kern[jnp.clip(-lag, 0, tb - 1), 1], 0.0)
    kloc = jnp.transpose(kf + kb, (2, 0, 4, 1, 3)).reshape(g, tb * p, tb * p)

    cf_re = jnp.transpose(e_re[1:tb + 1, 0], (1, 3, 0, 2)).reshape(g, n, tb * p)
    cf_im = -jnp.transpose(e_im[1:tb + 1, 0], (1, 3, 0, 2)).reshape(g, n, tb * p)
    cb_re = jnp.transpose(e_re[tb:0:-1, 1], (1, 3, 0, 2)).reshape(g, n, tb * p)
    cb_im = -jnp.transpose(e_im[tb:0:-1, 1], (1, 3, 0, 2)).reshape(g, n, tb * p)
    c4 = jnp.stack([cf_re, cf_im, cb_re, cb_im], axis=1)

    def state_in(pr, pi, br, bi):
        re = pr[..., None] * br[None] - pi[..., None] * bi[None]
        im = pr[..., None] * bi[None] + pi[..., None] * br[None]
        to_rows = lambda x: jnp.transpose(x, (1, 0, 3, 2)).reshape(g, tb * p, n)
        return to_rows(re), to_rows(im)

    sf_re, sf_im = state_in(pw_re[tb - 1::-1, 0][:tb], pw_im[tb - 1::-1, 0][:tb], bb_re[0], bb_im[0])
    sb_re, sb_im = state_in(pw_re[:tb, 1], pw_im[:tb, 1], bb_re[1], bb_im[1])
    b4 = jnp.stack([sf_re, sf_im, sb_re, sb_im], axis=2)

    eye2 = jnp.eye(2, dtype=F32)
    hp = g // 2
    ws = jnp.einsum("jgrcn,gh->jgrchn", b4.reshape(hp, 2, tb * p, 4, n), eye2)
    ws = ws.reshape(hp, 2 * tb * p, 8 * n)
    wyx = jnp.einsum("jgcnk,gh->jcgnhk", c4.reshape(hp, 2, 4, n, tb * p), eye2)
    wyx = wyx.reshape(hp, 8 * n, 2 * tb * p)
    wyu = jnp.einsum("jgrk,gh->jgrhk", kloc.reshape(hp, 2, tb * p, tb * p), eye2)
    wyu = wyu.reshape(hp, 2 * tb * p, 2 * tb * p)

    dec = jnp.stack([pw_re[tb, 0], pw_im[tb, 0], pw_re[tb, 1], pw_im[tb, 1]], axis=1)
    dec = jnp.transpose(dec.reshape(hp, 2, 4, n), (0, 2, 1, 3)).reshape(hp, 1, 8 * n)
    dsk = jnp.broadcast_to(d_skip.astype(F32).reshape(hp, 2, 1, p), (hp, 2, tb, p)).reshape(hp, 1, 2 * tb * p)
    return ws.astype(BF16), wyu.astype(BF16), wyx.astype(BF16), dec, dsk


def _s5_kernel(u_ref, ws_ref, wyu_ref, wyx_ref, dec_ref, dsk_ref, y_ref, s_ref, *, seq_rows, pb, rc):
    rows = u_ref.shape[0]
    w = ws_ref.shape[1]
    sn = ws_ref.shape[2] // 4
    n_rc = rows // rc

    for p in range(pb):
        def fill(c, carry, p=p):
            r0 = pl.multiple_of(c * rc, rc)
            u = u_ref[pl.ds(r0, rc), p * w:(p + 1) * w].astype(BF16)
            s_ref[pl.ds(r0, rc), p * w:(p + 1) * w] = jnp.dot(u, ws_ref[p], preferred_element_type=F32)
            return carry
        lax.fori_loop(0, n_rc, fill, 0)

    by_len = {}
    for start, length in seq_rows:
        by_len.setdefault(length, []).append(start)
    for length, starts in by_len.items():
        chains = [(st, p) for st in starts for p in range(pb)]
        decays = [tuple(dec_ref[p, :, c * sn:(c + 1) * sn] for c in range(4)) for p in range(pb)]

        def step(k8, carry, chains=chains, length=length, decays=decays):
            out = []
            for (st, p), (fr, fi, br, bi) in zip(chains, carry):
                dfr, dfi, dbr, dbi = decays[p]
                rf = pl.multiple_of(st + k8 * SUBLANES, SUBLANES)
                rb = pl.multiple_of(st + length - SUBLANES - k8 * SUBLANES, SUBLANES)
                cols = [slice(p * w + c * sn, p * w + (c + 1) * sn) for c in range(4)]
                sfr, sfi = s_ref[pl.ds(rf, SUBLANES), cols[0]], s_ref[pl.ds(rf, SUBLANES), cols[1]]
                sbr, sbi = s_ref[pl.ds(rb, SUBLANES), cols[2]], s_ref[pl.ds(rb, SUBLANES), cols[3]]
                xfr, xfi, xbr, xbi = [], [], [], []
                for k in range(SUBLANES):
                    xfr.append(fr)
                    xfi.append(fi)
                    fr, fi = dfr * fr - dfi * fi + sfr[k:k + 1], dfr * fi + dfi * fr + sfi[k:k + 1]
                for k in reversed(range(SUBLANES)):
                    xbr.insert(0, br)
                    xbi.insert(0, bi)
                    br, bi = dbr * br - dbi * bi + sbr[k:k + 1], dbr * bi + dbi * br + sbi[k:k + 1]
                s_ref[pl.ds(rf, SUBLANES), cols[0]] = jnp.concatenate(xfr, axis=0)
                s_ref[pl.ds(rf, SUBLANES), cols[1]] = jnp.concatenate(xfi, axis=0)
                s_ref[pl.ds(rb, SUBLANES), cols[2]] = jnp.concatenate(xbr, axis=0)
                s_ref[pl.ds(rb, SUBLANES), cols[3]] = jnp.concatenate(xbi, axis=0)
                out.append((fr, fi, br, bi))
            return tuple(out)

        assert length % SUBLANES == 0 and all(st % SUBLANES == 0 for st in starts)
        zero = jnp.zeros((1, sn), F32)
        lax.fori_loop(0, length // SUBLANES, step, tuple((zero, zero, zero, zero) for _ in chains))

    for p in range(pb):
        def emit(c, carry, p=p):
            r0 = pl.multiple_of(c * rc, rc)
            u = u_ref[pl.ds(r0, rc), p * w:(p + 1) * w]
            x = s_ref[pl.ds(r0, rc), p * w:(p + 1) * w]
            y = (jnp.dot(u.astype(BF16), wyu_ref[p], preferred_element_type=F32)
                 + jnp.dot(x.astype(BF16), wyx_ref[p], preferred_element_type=F32)
                 + dsk_ref[p] * u)
            y_ref[pl.ds(r0, rc), p * w:(p + 1) * w] = jax.nn.gelu(y).astype(y_ref.dtype)
            return carry
        lax.fori_loop(0, n_rc, emit, 0)


def s5_core(u_blk, ops, seq_rows):
    ws, wyu, wyx, dec, dsk = ops
    rows, width = u_blk.shape
    hp, w, s4 = ws.shape
    pb = S5_PAIR_BLOCK if hp % S5_PAIR_BLOCK == 0 else 1
    rc = _tile(rows, S5_ROW_CHUNK, 8)
    kern = functools.partial(_s5_kernel, seq_rows=tuple(seq_rows), pb=pb, rc=rc)
    wspec = lambda a, b: pl.BlockSpec((pb, a, b), lambda j: (j, 0, 0))
    return pl.pallas_call(
        kern,
        name="s5_core",
        grid=(hp // pb,),
        in_specs=[pl.BlockSpec((rows, pb * w), lambda j: (0, j)),
                  wspec(w, s4), wspec(w, w), wspec(s4, w), wspec(1, s4), wspec(1, w)],
        out_specs=pl.BlockSpec((rows, pb * w), lambda j: (0, j)),
        out_shape=jax.ShapeDtypeStruct((rows, width), BF16),
        scratch_shapes=[pltpu.VMEM((rows, pb * w), F32)],
        compiler_params=_params(("parallel",)),
    )(u_blk, ws, wyu, wyx, dec, dsk)


def s5_layer(x, seqs, mix_g, in_w, ssm, glu_w, glu_b, out_w):
    t, d = x.shape
    e = glu_w.shape[0]
    g, p = ssm[3].shape[1], ssm[3].shape[-1]
    tb = S5_TB
    h = rmsnorm(x, mix_g, BF16)
    uz = matmul(h, in_w.astype(BF16), out_dtype=F32)
    ops = _s5_operators(*ssm)
    u_blk = uz[:, :e].reshape(t // tb, tb, g // 2, 2, p)
    u_blk = jnp.transpose(u_blk, (0, 2, 3, 1, 4)).reshape(t // tb, g * tb * p)
    seq_rows = [(s // tb, l // tb) for s, l in seqs]
    y_blk = s5_core(u_blk, ops, seq_rows)
    y = jnp.transpose(y_blk.reshape(t // tb, g // 2, 2, tb, p), (0, 3, 1, 2, 4)).reshape(t, e)
    gated = matmul(y, glu_w.astype(BF16), out_dtype=BF16, mode="glu",
                   extra=(glu_b.reshape(1, e).astype(F32), y, uz),
                   extra_specs=(_row_spec, _tile_spec(), _tile_spec(e)), tm=512, tn=1024)
    return matmul(gated, out_w.astype(BF16), out_dtype=F32, mode="residual",
                  extra=(x,), extra_specs=(_tile_spec(),), tn=512)


def _conv_qkv_kernel(flags_ref, xm_ref, prev_ref, next_ref, cw_ref, cb_ref, wq_ref, wk_ref, wkt_ref, wv_ref,
                     xc_ref, q_ref, k_ref, kt_ref, v_ref, pad_ref, *, ksize, k_scale):
    i = pl.program_id(0)
    tm = xm_ref.shape[0]
    half = ksize // 2
    xm = xm_ref[...]
    pad_ref[0:8, :] = prev_ref[...] * flags_ref[0, i].astype(F32)
    pad_ref[8:8 + tm, :] = xm
    pad_ref[8 + tm:16 + tm, :] = next_ref[...] * flags_ref[1, i].astype(F32)
    acc = cb_ref[...] + jnp.zeros_like(xm)
    for k in range(ksize):
        acc = acc + pad_ref[8 + k - half:8 + k - half + tm, :] * cw_ref[k:k + 1, :]
    xc = _silu(acc)
    xc_ref[...] = xc
    xcb = xc.astype(BF16)
    q_ref[...] = jnp.dot(xcb, wq_ref[0], preferred_element_type=F32).astype(q_ref.dtype)
    k_ref[...] = (jnp.dot(xcb, wk_ref[0], preferred_element_type=F32) * k_scale).astype(k_ref.dtype)
    kt = lax.dot_general(wkt_ref[0], xcb, (((1,), (1,)), ((), ())), preferred_element_type=F32)
    kt_ref[...] = (kt * k_scale).astype(kt_ref.dtype)
    v_ref[...] = jnp.dot(xm.astype(BF16), wv_ref[0], preferred_element_type=F32).astype(v_ref.dtype)


def _dense_block_diag(wb, width):
    nb, b, _ = wb.shape
    per = width // b
    tiles = wb.reshape(nb // per, per, b, b)
    eye = jnp.eye(per, dtype=wb.dtype)
    return jnp.einsum("tncd,nm->tncmd", tiles, eye).reshape(nb // per, width, width)


def conv_qkv(proj, seqs, conv_w, conv_b, wq, wk, wv, e, dk):
    t = proj.shape[0]
    ksize = conv_w.shape[0]
    cw = _tile(e, QKV_DENSE, LANES)
    lmin = functools.reduce(math.gcd, [l for _, l in seqs])
    tm = _tile(lmin, 512, LANES)
    nt = t // tm
    starts = {s for s, _ in seqs}
    ends = {s + l for s, l in seqs}
    flags = jnp.array([[0 if i * tm in starts else 1 for i in range(nt)],
                       [0 if (i + 1) * tm in ends else 1 for i in range(nt)]], jnp.int32)
    wqd = _dense_block_diag(wq, cw).astype(BF16)
    wkd = _dense_block_diag(wk, cw).astype(BF16)
    wktd = jnp.swapaxes(wkd, 1, 2)
    wvd = _dense_block_diag(wv, cw).astype(BF16)
    r8 = tm // 8
    last8 = t // 8 - 1
    kern = functools.partial(_conv_qkv_kernel, ksize=ksize, k_scale=float(dk) ** -0.5)
    wspec = pl.BlockSpec((1, cw, cw), lambda i, c, f: (c, 0, 0))
    tile = pl.BlockSpec((tm, cw), lambda i, c, f: (i, c))
    outs = pl.pallas_call(
        kern,
        name="conv_qkv",
        grid_spec=pltpu.PrefetchScalarGridSpec(
            num_scalar_prefetch=1,
            grid=(nt, e // cw),
            in_specs=[tile,
                      pl.BlockSpec((8, cw), lambda i, c, f: (jnp.maximum(i * r8 - 1, 0), c)),
                      pl.BlockSpec((8, cw), lambda i, c, f: (jnp.minimum((i + 1) * r8, last8), c)),
                      pl.BlockSpec((ksize, cw), lambda i, c, f: (0, c)),
                      pl.BlockSpec((1, cw), lambda i, c, f: (0, c)),
                      wspec, wspec, wspec, wspec],
            out_specs=[tile, tile, tile, pl.BlockSpec((cw, tm), lambda i, c, f: (c, i)), tile],
            scratch_shapes=[pltpu.VMEM((tm + 16, cw), F32)]),
        out_shape=[jax.ShapeDtypeStruct((t, e), F32), jax.ShapeDtypeStruct((t, e), BF16),
                   jax.ShapeDtypeStruct((t, e), BF16), jax.ShapeDtypeStruct((e, t), BF16),
                   jax.ShapeDtypeStruct((t, e), BF16)],
        compiler_params=_params(("parallel", "parallel")),
    )(flags, proj, proj, proj, conv_w.astype(F32), conv_b.reshape(1, e).astype(F32), wqd, wkd, wktd, wvd)
    return outs


def _log_sigmoid(x):
    return jnp.minimum(x, 0.0) - jnp.log(1.0 + jnp.exp(-jnp.abs(x)))


def _mlstm_kernel(reset_ref, q_ref, k_ref, kt_ref, v_ref, ic_ref, fc_ref, ir_ref, fr_ref, h_ref,
                  c_ref, n_ref, m_ref, *, hb, dk):
    d = pl.program_id(0)
    c = pl.program_id(2)
    ch = q_ref.shape[0]
    hi = lax.Precision.HIGHEST

    @pl.when(reset_ref[d, c] == 1)
    def _():
        c_ref[...] = jnp.zeros_like(c_ref)
        n_ref[...] = jnp.zeros_like(n_ref)
        m_ref[...] = jnp.zeros_like(m_ref)

    tt = lax.broadcasted_iota(jnp.int32, (ch, ch), 0)
    ss = lax.broadcasted_iota(jnp.int32, (ch, ch), 1)
    sign = 1 - 2 * d
    allowed = (ss - tt) * sign <= 0
    tri = jnp.where(allowed, 1.0, 0.0).astype(F32)
    tri_t = jnp.where((tt - ss) * sign <= 0, 1.0, 0.0).astype(F32)

    lf_col = _log_sigmoid(fc_ref[0, 0])
    lf_row = _log_sigmoid(fr_ref[0, 0])
    bcum_col = jnp.dot(tri, lf_col, preferred_element_type=F32, precision=hi)
    bcum_row = jnp.dot(lf_row, tri_t, preferred_element_type=F32, precision=hi)
    i_col = ic_ref[0, 0]
    i_row = ir_ref[0, 0]
    blast_all = jnp.sum(lf_row, axis=-1, keepdims=True)

    for j in range(hb):
        lanes = slice(j * dk, (j + 1) * dk)
        bc = bcum_col[:, j:j + 1]
        ic = i_col[:, j:j + 1]
        br = bcum_row[j:j + 1, :]
        ir = i_row[j:j + 1, :]
        m_prev = m_ref[j, 0:1, 0:1]
        b_last = blast_all[j:j + 1, :]

        dmat = jnp.where(allowed, bc - br + ir, NEG_BIG)
        inter = bc + m_prev
        m_t = jnp.maximum(inter, jnp.max(dmat, axis=-1, keepdims=True))
        w_inter = jnp.exp(inter - m_t)
        qj = q_ref[:, lanes]
        s = jnp.dot(qj, kt_ref[lanes, :], preferred_element_type=F32) * jnp.exp(dmat - m_t)
        cmat = c_ref[j]
        nrow = n_ref[j]
        num = (w_inter * jnp.dot(qj, cmat.astype(BF16), preferred_element_type=F32)
               + jnp.dot(s.astype(BF16), v_ref[:, lanes], preferred_element_type=F32))
        den = (w_inter * jnp.sum(qj.astype(F32) * nrow, axis=-1, keepdims=True)
               + jnp.sum(s, axis=-1, keepdims=True))
        h_ref[0, :, lanes] = num / jnp.maximum(jnp.abs(den), jnp.exp(-m_t))

        g_row = b_last - br + ir
        m_new = jnp.maximum(b_last + m_prev, jnp.max(g_row, axis=-1, keepdims=True))
        decay = jnp.exp(b_last + m_prev - m_new)
        wg_row = jnp.exp(g_row - m_new)
        wg_col = jnp.exp(b_last - bc + ic - m_new)
        kts = (kt_ref[lanes, :].astype(F32) * wg_row).astype(BF16)
        c_ref[j] = decay * cmat + jnp.dot(kts, v_ref[:, lanes], preferred_element_type=F32)
        n_ref[j] = decay * nrow + jnp.sum(k_ref[:, lanes].astype(F32) * wg_col, axis=0, keepdims=True)
        m_ref[j] = jnp.broadcast_to(m_new, m_ref.shape[1:])


def mlstm_core(q, k, kt, v, gates, seqs, nh):
    t, e = q.shape
    dk = e // nh
    hb = MLSTM_HEAD_BLOCK if nh % MLSTM_HEAD_BLOCK == 0 else nh
    nhb = nh // hb
    lmin = functools.reduce(math.gcd, [l for _, l in seqs])
    ch = _tile(lmin, MLSTM_CHUNK, LANES)
    nc = t // ch
    starts = {s for s, _ in seqs}
    ends = {s + l for s, l in seqs}
    reset = jnp.array([[1 if c * ch in starts else 0 for c in range(nc)],
                       [1 if (nc - c) * ch in ends else 0 for c in range(nc)]], jnp.int32)
    g4 = gates[:, :4 * nh].reshape(t, 4, nhb, hb)
    g_row = jnp.transpose(g4, (1, 2, 3, 0))
    g_col = jnp.pad(jnp.transpose(g4, (1, 2, 0, 3)), ((0, 0), (0, 0), (0, 0), (0, LANES - hb)))

    cidx = lambda d, c: c + d * (nc - 1 - 2 * c)
    tile = pl.BlockSpec((ch, hb * dk), lambda d, h, c, r: (cidx(d, c), h))
    colspec = lambda off: pl.BlockSpec((1, 1, ch, LANES), lambda d, h, c, r: (2 * d + off, h, cidx(d, c), 0))
    rowspec = lambda off: pl.BlockSpec((1, 1, hb, ch), lambda d, h, c, r: (2 * d + off, h, 0, cidx(d, c)))
    kern = functools.partial(_mlstm_kernel, hb=hb, dk=dk)
    return pl.pallas_call(
        kern,
        name="mlstm_core",
        grid_spec=pltpu.PrefetchScalarGridSpec(
            num_scalar_prefetch=1,
            grid=(2, nhb, nc),
            in_specs=[tile, tile,
                      pl.BlockSpec((hb * dk, ch), lambda d, h, c, r: (h, cidx(d, c))),
                      tile, colspec(0), colspec(1), rowspec(0), rowspec(1)],
            out_specs=pl.BlockSpec((1, ch, hb * dk), lambda d, h, c, r: (d, cidx(d, c), h)),
            scratch_shapes=[pltpu.VMEM((hb, dk, dk), F32), pltpu.VMEM((hb, 1, dk), F32),
                            pltpu.VMEM((hb, 8, LANES), F32)]),
        out_shape=jax.ShapeDtypeStruct((2, t, e), F32),
        compiler_params=_params(("parallel", "parallel", "arbitrary")),
    )(reset, q, k, kt, v, g_col, g_col, g_row, g_row)


def _mlstm_post_kernel(hf_ref, hb_ref, o_ref, z_ref, xc_ref, hn_ref, sk_ref, out_ref):
    hs = hf_ref[0] + hb_ref[0]
    hs = hs * lax.rsqrt(jnp.mean(hs * hs, axis=-1, keepdims=True) + EPS)
    hs = hs * hn_ref[...]
    hs = _sigmoid(o_ref[...]) * hs + sk_ref[...] * xc_ref[...]
    out_ref[...] = (hs * _silu(z_ref[...])).astype(out_ref.dtype)


def mlstm_post(hdir, proj, xc, head_norm, skip, e, nh):
    t = xc.shape[0]
    dk = e // nh
    tm = _tile(t, 1024, 8)
    hspec = lambda d: pl.BlockSpec((1, tm, dk), lambda i, h: (d, i, h))
    pspec = lambda off: pl.BlockSpec((tm, dk), lambda i, h: (i, h + off * nh))
    vec = pl.BlockSpec((1, dk), lambda i, h: (0, h))
    return pl.pallas_call(
        _mlstm_post_kernel,
        name="mlstm_post",
        grid=(t // tm, nh),
        in_specs=[hspec(0), hspec(1), pspec(2), pspec(1), pspec(0), vec, vec],
        out_specs=pl.BlockSpec((tm, dk), lambda i, h: (i, h)),
        out_shape=jax.ShapeDtypeStruct((t, e), BF16),
        compiler_params=_params(("parallel", "parallel")),
    )(hdir, hdir, proj, proj, xc, head_norm.reshape(1, e).astype(F32), skip.reshape(1, e).astype(F32))


def mlstm_layer(x, seqs, mix_g, in_w, gate_b, conv_w, conv_b, wq, wk, wv, skip, head_norm, out_w):
    e = out_w.shape[0]
    nh = gate_b.shape[0] // 4
    h = rmsnorm(x, mix_g, BF16)
    proj = matmul(h, in_w[:, :3 * e].astype(BF16), out_dtype=F32)
    gw = jnp.pad(in_w[:, 3 * e:], ((0, 0), (0, LANES - 4 * nh))).astype(BF16)
    gb = jnp.pad(gate_b, (0, LANES - 4 * nh)).reshape(1, LANES).astype(F32)
    gates = matmul(h, gw, out_dtype=F32, mode="bias", extra=(gb,), extra_specs=(_row_spec,))
    xc, q, k, kt, v = conv_qkv(proj, seqs, conv_w, conv_b, wq, wk, wv, e, e // nh)
    hdir = mlstm_core(q, k, kt, v, gates, seqs, nh)
    mixed = mlstm_post(hdir, proj, xc, head_norm, skip, e, nh)
    return matmul(mixed, out_w.astype(BF16), out_dtype=F32, mode="residual",
                  extra=(x,), extra_specs=(_tile_spec(),), tn=512)


def _xattn_kernel(seq_ref, q_ref, k_ref, v_ref, o_ref, *, scale):
    del seq_ref
    s = lax.dot_general(q_ref[...], k_ref[0], (((1,), (1,)), ((), ())), preferred_element_type=F32) * scale
    s = s - jnp.max(s, axis=-1, keepdims=True)
    p = jnp.exp(s)
    p = p / jnp.sum(p, axis=-1, keepdims=True)
    o_ref[...] = jnp.dot(p.astype(BF16), v_ref[0], preferred_element_type=F32).astype(o_ref.dtype)


def xattn_core(q, kmem, vmem, seqs):
    t, d = q.shape
    hd = d // XA_HEADS
    lmin = functools.reduce(math.gcd, [l for _, l in seqs])
    tm = _tile(lmin, 512, 8)
    seq_of = jnp.array([max(j for j, (s, _) in enumerate(seqs) if s <= i * tm) for i in range(t // tm)],
                       jnp.int32)
    m = kmem.shape[1]
    kern = functools.partial(_xattn_kernel, scale=float(hd) ** -0.5)
    mem_spec = pl.BlockSpec((1, m, hd), lambda i, h, sq: (sq[i], 0, h))
    return pl.pallas_call(
        kern,
        name="xattn_core",
        grid_spec=pltpu.PrefetchScalarGridSpec(
            num_scalar_prefetch=1,
            grid=(t // tm, XA_HEADS),
            in_specs=[pl.BlockSpec((tm, hd), lambda i, h, sq: (i, h)), mem_spec, mem_spec],
            out_specs=pl.BlockSpec((tm, hd), lambda i, h, sq: (i, h))),
        out_shape=jax.ShapeDtypeStruct((t, d), BF16),
        compiler_params=_params(("parallel", "parallel")),
    )(seq_of, q, kmem, vmem)


def xattn_layer(x, mem, seqs, xa_g, mem_g, wq, wk, wv, wo):
    ns, m, d = mem.shape
    hq = rmsnorm(x, xa_g, BF16)
    mem_n = rmsnorm(mem.reshape(ns * m, d), mem_g, BF16)
    q = matmul(hq, wq.astype(BF16), out_dtype=BF16)
    kmem = matmul(mem_n, wk.astype(BF16), out_dtype=BF16).reshape(ns, m, d)
    vmem = matmul(mem_n, wv.astype(BF16), out_dtype=BF16).reshape(ns, m, d)
    o = xattn_core(q, kmem, vmem, seqs)
    return matmul(o, wo.astype(BF16), out_dtype=F32, mode="residual", extra=(x,), extra_specs=(_tile_spec(),))


def kernel(x_prompt, x_sample, mem_prompt, mem_sample, mix_norm, a_in_w, s5_a_re, s5_a_im, s5_log_step,
           s5_b_re, s5_b_im, s5_c_re, s5_c_im, s5_d, a_glu_w, a_glu_b, a_out_w, b_in_w, b_gate_bias,
           b_conv_w, b_conv_b, b_wq, b_wk, b_wv, b_skip, b_head_norm, b_out_w, xa_norm, mem_norm,
           xa_wq, xa_wk, xa_wv, xa_wo, final_norm):
    bp, lp, d = x_prompt.shape
    bs, ls, _ = x_sample.shape
    depth = mix_norm.shape[0]
    seqs = [(i * lp, lp) for i in range(bp)] + [(bp * lp + i * ls, ls) for i in range(bs)]
    x = jnp.concatenate([x_prompt.reshape(bp * lp, d), x_sample.reshape(bs * ls, d)], axis=0)
    mem = jnp.concatenate([mem_prompt, mem_sample], axis=0)
    for i in range(depth):
        j = i // 2
        if i % 2 == 0:
            ssm = (s5_a_re[j], s5_a_im[j], s5_log_step[j], s5_b_re[j], s5_b_im[j], s5_c_re[j], s5_c_im[j],
                   s5_d[j])
            x = s5_layer(x, seqs, mix_norm[i], a_in_w[j], ssm, a_glu_w[j], a_glu_b[j], a_out_w[j])
        else:
            x = mlstm_layer(x, seqs, mix_norm[i], b_in_w[j], b_gate_bias[j], b_conv_w[j], b_conv_b[j],
                            b_wq[j], b_wk[j], b_wv[j], b_skip[j], b_head_norm[j], b_out_w[j])
        x = xattn_layer(x, mem, seqs, xa_norm[i], mem_norm[i], xa_wq[i], xa_wk[i], xa_wv[i], xa_wo[i])
    y = rmsnorm(x, final_norm, F32)
    return (y[:bp * lp].reshape(bp, lp, d), y[bp * lp:].reshape(bs, ls, d))
```

```python
import functools
import math

import jax
import jax.numpy as jnp
from jax import lax
from jax.experimental import pallas as pl
from jax.experimental.pallas import tpu as pltpu

F32 = jnp.float32
BF16 = jnp.bfloat16

XA_HEADS = 4
EPS = 1e-6
S5_TB = 16
S5_PAIR_BLOCK = 2
S5_ROW_CHUNK = 256
S5_RELAYOUT_ROWS = 128
S5_RELAYOUT_LANES = 128
S5_RELAYOUT_CHUNK = 32
MLSTM_CHUNK = 256
MLSTM_HEAD_BLOCK = 8
QKV_DENSE = 256
LANES = 128
SUBLANES = 8
V7X_VMEM_LIMIT = 56 * 1024 * 1024
NEG_BIG = -1e30


def _tile(n, pref, mult):
    best = None
    for t in range(mult, min(n, pref) + 1, mult):
        if n % t == 0:
            best = t
    return n if best is None else best


def _params(sem):
    return pltpu.CompilerParams(dimension_semantics=sem, vmem_limit_bytes=V7X_VMEM_LIMIT)


def _rmsnorm_kernel(x_ref, g_ref, o_ref):
    x = x_ref[...].astype(F32)
    ms = jnp.mean(x * x, axis=-1, keepdims=True)
    o_ref[...] = (x * lax.rsqrt(ms + EPS) * g_ref[...]).astype(o_ref.dtype)


def rmsnorm(x, g, out_dtype):
    t, d = x.shape
    tm = _tile(t, 512, 8)
    return pl.pallas_call(
        _rmsnorm_kernel,
        name="rmsnorm",
        grid=(t // tm,),
        in_specs=[pl.BlockSpec((tm, d), lambda i: (i, 0)),
                  pl.BlockSpec((1, d), lambda i: (0, 0))],
        out_specs=pl.BlockSpec((tm, d), lambda i: (i, 0)),
        out_shape=jax.ShapeDtypeStruct((t, d), out_dtype),
        compiler_params=_params(("parallel",)),
    )(x, g.reshape(1, d).astype(F32))


def _sigmoid(x):
    return 1.0 / (1.0 + jnp.exp(-x))


def _silu(x):
    return x * _sigmoid(x)


def _mm_epilogue(mode, acc, extra):
    if mode == "plain":
        return acc
    if mode == "bias":
        return acc + extra[0][...]
    if mode == "scale":
        return acc * extra[0][...]
    if mode == "residual":
        return extra[0][...] + acc
    if mode == "glu":
        b_ref, y_ref, z_ref = extra
        y = y_ref[...].astype(F32)
        return y * _sigmoid(acc + b_ref[...]) * _silu(z_ref[...])
    raise ValueError(mode)


def _mm_kernel(*refs, nk, mode, n_extra):
    a_ref, w_ref = refs[0], refs[1]
    extra = refs[2:2 + n_extra]
    o_ref = refs[2 + n_extra]
    part = jnp.dot(a_ref[...], w_ref[...], preferred_element_type=F32)
    if nk == 1:
        o_ref[...] = _mm_epilogue(mode, part, extra).astype(o_ref.dtype)
        return
    acc_ref = refs[3 + n_extra]
    k = pl.program_id(2)

    @pl.when(k == 0)
    def _():
        acc_ref[...] = part

    @pl.when(jnp.logical_and(k > 0, k < nk - 1))
    def _():
        acc_ref[...] += part

    @pl.when(k == nk - 1)
    def _():
        o_ref[...] = _mm_epilogue(mode, acc_ref[...] + part, extra).astype(o_ref.dtype)


def matmul(a, w, *, out_dtype, mode="plain", extra=(), extra_specs=(), tm=1024, tn=1024, tk=4096):
    m, kd = a.shape
    n = w.shape[1]
    tm, tn, tk = _tile(m, tm, 8), _tile(n, tn, LANES), _tile(kd, tk, LANES)
    nk = kd // tk
    kern = functools.partial(_mm_kernel, nk=nk, mode=mode, n_extra=len(extra))
    in_specs = [pl.BlockSpec((tm, tk), lambda i, j, k: (i, k)),
                pl.BlockSpec((tk, tn), lambda i, j, k: (k, j))]
    in_specs += [mk(tm, tn) for mk in extra_specs]
    return pl.pallas_call(
        kern,
        name="matmul_" + mode,
        grid=(m // tm, n // tn, nk),
        in_specs=in_specs,
        out_specs=pl.BlockSpec((tm, tn), lambda i, j, k: (i, j)),
        out_shape=jax.ShapeDtypeStruct((m, n), out_dtype),
        scratch_shapes=[pltpu.VMEM((tm, tn), F32)] if nk > 1 else [],
        compiler_params=_params(("parallel", "parallel", "arbitrary")),
    )(a, w, *extra)


def _row_spec(tm, tn):
    return pl.BlockSpec((1, tn), lambda i, j, k: (0, j))


def _tile_spec(col_off=0):
    def mk(tm, tn):
        off = col_off // tn
        return pl.BlockSpec((tm, tn), lambda i, j, k: (i, j + off))
    return mk


def _s5_operators(a_re, a_im, log_step, b_re, b_im, c_re, c_im):
    hi = lax.Precision.HIGHEST
    tb = S5_TB
    _, g, n = a_re.shape
    p = b_re.shape[-1]
    hp, m = g // 2, tb * p
    lam_re = jnp.minimum(a_re.astype(F32), -1e-4)
    lam_im = a_im.astype(F32)
    dt = jnp.exp(log_step.astype(F32))[..., None]
    mag = jnp.exp(lam_re * dt)
    ab_re = mag * jnp.cos(lam_im * dt)
    ab_im = mag * jnp.sin(lam_im * dt)
    den = lam_re * lam_re + lam_im * lam_im
    nr, ni = ab_re - 1.0, ab_im
    f_re = (nr * lam_re + ni * lam_im) / den
    f_im = (ni * lam_re - nr * lam_im) / den
    b_re, b_im = b_re.astype(F32), b_im.astype(F32)
    bb_re = f_re[..., None] * b_re - f_im[..., None] * b_im
    bb_im = f_re[..., None] * b_im + f_im[..., None] * b_re
    c_re, c_im = c_re.astype(F32), c_im.astype(F32)

    def powers(steps):
        pmag = jnp.exp((lam_re * dt)[..., None] * steps)
        pang = (lam_im * dt)[..., None] * steps
        return pmag * jnp.cos(pang), pmag * jnp.sin(pang)

    pw_re, pw_im = powers(jnp.arange(tb + 1, dtype=F32))
    rw_re, rw_im = powers(jnp.arange(SUBLANES + 1, dtype=F32) * tb)

    def c_times_powers(cr, ci, pr, pi):
        ct_re = jnp.tile(jnp.swapaxes(cr, 1, 2), (1, 1, tb + 1))
        ct_im = jnp.tile(jnp.swapaxes(ci, 1, 2), (1, 1, tb + 1))
        rp_re, rp_im = jnp.repeat(pr, p, axis=-1), jnp.repeat(pi, p, axis=-1)
        return ct_re * rp_re - ct_im * rp_im, ct_re * rp_im + ct_im * rp_re

    ef_re, ef_im = c_times_powers(c_re[0], c_im[0], pw_re[0], pw_im[0])
    eb_re, eb_im = c_times_powers(c_re[1], c_im[1], pw_re[1][..., ::-1], pw_im[1][..., ::-1])

    def lag_rows(br, bi, er, ei):
        return (jnp.einsum("gnq,gnm->gqm", br, er, precision=hi)
                - jnp.einsum("gnq,gnm->gqm", bi, ei, precision=hi))

    kt_f = lag_rows(bb_re[0], bb_im[0], ef_re[..., :m], ef_im[..., :m])
    kt_b = lag_rows(bb_re[1], bb_im[1], eb_re[..., p:], eb_im[..., p:])
    zpad = jnp.zeros_like(kt_f)
    kf_pad = jnp.concatenate([zpad, kt_f], axis=-1)
    kb_pad = jnp.concatenate([kt_b, zpad], axis=-1)
    kloc = jnp.stack([kf_pad[..., m - p * s:2 * m - p * s] + kb_pad[..., (tb - 1 - s) * p:(tb - 1 - s) * p + m]
                      for s in range(tb)], axis=1).reshape(g, m, m)

    cf_re, cf_im = ef_re[..., p:], -ef_im[..., p:]
    cb_re, cb_im = eb_re[..., :m], -eb_im[..., :m]

    def state_in(pr, pi, br, bi):
        pr_t, pi_t = jnp.swapaxes(pr, 1, 2)[:, :, None, :], jnp.swapaxes(pi, 1, 2)[:, :, None, :]
        br_t, bi_t = jnp.swapaxes(br, 1, 2)[:, None, :, :], jnp.swapaxes(bi, 1, 2)[:, None, :, :]
        return ((pr_t * br_t - pi_t * bi_t).reshape(g, m, n), (pr_t * bi_t + pi_t * br_t).reshape(g, m, n))

    sf_re, sf_im = state_in(pw_re[0][..., tb - 1::-1], pw_im[0][..., tb - 1::-1], bb_re[0], bb_im[0])
    sb_re, sb_im = state_in(pw_re[1][..., :tb], pw_im[1][..., :tb], bb_re[1], bb_im[1])

    def block_diag2(a):
        a = a.reshape((hp, 2) + a.shape[1:])
        z = jnp.zeros_like(a[:, 0])
        return jnp.concatenate([jnp.concatenate([a[:, 0], z], axis=-1),
                                jnp.concatenate([z, a[:, 1]], axis=-1)], axis=1)

    wyu = block_diag2(kloc)
    ws = jnp.concatenate([block_diag2(c) for c in (sf_re, sf_im, sb_re, sb_im)], axis=-1)
    wyx = jnp.concatenate([block_diag2(c) for c in (cf_re, cf_im, cb_re, cb_im)], axis=1)

    def pair_rows(a):
        return jnp.transpose(a.reshape(hp, 2, n, a.shape[-1]), (0, 3, 1, 2)).reshape(hp, a.shape[-1], 2 * n)

    asc = jnp.concatenate([pair_rows(c) for c in (rw_re[0], rw_im[0], rw_re[1], rw_im[1])], axis=-1)
    dec = jnp.concatenate([asc, jnp.zeros((hp, 2 * SUBLANES - asc.shape[1], 8 * n), F32),
                           asc[:, SUBLANES - 1::-1]], axis=1)
    return ws.astype(BF16), wyu.astype(BF16), wyx.astype(BF16), dec


def _s5_pack_kernel(u_ref, o_ref, *, tb, p):
    gpv = LANES // p
    halves = tb // gpv
    rs = S5_RELAYOUT_CHUNK
    slot = lax.broadcasted_iota(jnp.int32, (rs, LANES), 1) // p
    for r0 in range(0, o_ref.shape[0], rs):
        for cb in range(u_ref.shape[1] // LANES):
            for c in range(halves):
                xs = [u_ref[pl.ds(r0 * tb + c * gpv + j, rs, stride=tb), cb * LANES:(cb + 1) * LANES]
                      for j in range(gpv)]
                for g in range(gpv):
                    acc = None
                    for j in range(gpv):
                        r = xs[j] if j == g else pltpu.roll(xs[j], (p * (j - g)) % LANES, axis=1)
                        acc = r if acc is None else jnp.where(slot == j, r, acc)
                    lane0 = ((cb * gpv + g) * halves + c) * LANES
                    o_ref[r0:r0 + rs, lane0:lane0 + LANES] = acc.astype(o_ref.dtype)


def _s5_unpack_kernel(y_ref, u_ref, d_ref, o_ref, nat_ref, *, tb, p):
    gpv = LANES // p
    halves = tb // gpv
    rs = S5_RELAYOUT_CHUNK
    slot = lax.broadcasted_iota(jnp.int32, (rs, LANES), 1) // p
    for r0 in range(0, y_ref.shape[0], rs):
        for cb in range(u_ref.shape[1] // LANES):
            lanes = slice(cb * LANES, (cb + 1) * LANES)
            for c in range(halves):
                chunks = [y_ref[r0:r0 + rs, ((cb * gpv + g) * halves + c) * LANES:((cb * gpv + g) * halves + c + 1) * LANES]
                          for g in range(gpv)]
                for j in range(gpv):
                    acc = None
                    for g in range(gpv):
                        r = chunks[g] if g == j else pltpu.roll(chunks[g], (p * (g - j)) % LANES, axis=1)
                        acc = r if acc is None else jnp.where(slot == g, r, acc)
                    rows = pl.ds(r0 * tb + c * gpv + j, rs, stride=tb)
                    nat_ref[rows, lanes] = jax.nn.gelu(acc + d_ref[:, lanes] * u_ref[rows, lanes])
    o_ref[...] = nat_ref[...].astype(o_ref.dtype)


def _relayout_blocks(t, e, tb):
    rb = _tile(t // tb, S5_RELAYOUT_ROWS, S5_RELAYOUT_CHUNK)
    w = _tile(e, S5_RELAYOUT_LANES, LANES)
    assert rb % S5_RELAYOUT_CHUNK == 0
    return rb, w


def s5_pack(uz, e, p):
    t, tb = uz.shape[0], S5_TB
    rb, w = _relayout_blocks(t, e, tb)
    return pl.pallas_call(
        functools.partial(_s5_pack_kernel, tb=tb, p=p),
        name="s5_pack",
        grid=(t // tb // rb, e // w),
        in_specs=[pl.BlockSpec((rb * tb, w), lambda i, j: (i, j))],
        out_specs=pl.BlockSpec((rb, w * tb), lambda i, j: (i, j)),
        out_shape=jax.ShapeDtypeStruct((t // tb, e * tb), BF16),
        compiler_params=_params(("parallel", "parallel")),
    )(uz)


def s5_unpack(y_blk, uz, d_skip, e, p):
    t, tb = uz.shape[0], S5_TB
    rb, w = _relayout_blocks(t, e, tb)
    return pl.pallas_call(
        functools.partial(_s5_unpack_kernel, tb=tb, p=p),
        name="s5_unpack",
        grid=(t // tb // rb, e // w),
        in_specs=[pl.BlockSpec((rb, w * tb), lambda i, j: (i, j)),
                  pl.BlockSpec((rb * tb, w), lambda i, j: (i, j)),
                  pl.BlockSpec((1, w), lambda i, j: (0, j))],
        out_specs=pl.BlockSpec((rb * tb, w), lambda i, j: (i, j)),
        out_shape=jax.ShapeDtypeStruct((t, e), BF16),
        scratch_shapes=[pltpu.VMEM((rb * tb, w), F32)],
        compiler_params=_params(("parallel", "parallel")),
    )(y_blk, uz, d_skip.reshape(1, e).astype(F32))


def _cmul(ar, ai, br, bi):
    return ar * br - ai * bi, ar * bi + ai * br


def _tile_scan(sr, si, xr, xi, d1, d2, d4, d8, pk, row, reverse):
    def shifted(v, dist):
        if reverse:
            return jnp.where(row < SUBLANES - dist, pltpu.roll(v, SUBLANES - dist, axis=0), 0.0)
        return jnp.where(row >= dist, pltpu.roll(v, dist, axis=0), 0.0)

    ir, ii = sr, si
    for dist, (dr, di) in ((1, d1), (2, d2), (4, d4)):
        mr, mi = _cmul(dr, di, shifted(ir, dist), shifted(ii, dist))
        ir, ii = ir + mr, ii + mi
    cr, ci = _cmul(pk[0], pk[1], xr, xi)
    orr, oi = _cmul(d8[0], d8[1], xr, xi)
    last = 0 if reverse else SUBLANES - 1
    return (shifted(ir, 1) + cr, shifted(ii, 1) + ci,
            ir[last:last + 1] + orr, ii[last:last + 1] + oi)


def _s5_kernel(u_ref, ws_ref, wyu_ref, wyx_ref, dec_ref, y_ref, s_ref, *, seq_rows, pb, rc):
    rows = u_ref.shape[0]
    w = ws_ref.shape[1]
    sn = ws_ref.shape[2] // 4
    n_rc = rows // rc

    for p in range(pb):
        def fill(c, carry, p=p):
            r0 = pl.multiple_of(c * rc, rc)
            u = u_ref[pl.ds(r0, rc), p * w:(p + 1) * w]
            s_ref[pl.ds(r0, rc), p * w:(p + 1) * w] = jnp.dot(u, ws_ref[p], preferred_element_type=F32)
            return carry
        lax.fori_loop(0, n_rc, fill, 0)

    row = lax.broadcasted_iota(jnp.int32, (SUBLANES, sn), 0)

    def tables(p, c_re, rows0):
        re, im = slice(c_re * sn, (c_re + 1) * sn), slice((c_re + 1) * sn, (c_re + 2) * sn)
        at = lambda k: (dec_ref[p, k:k + 1, re], dec_ref[p, k:k + 1, im])
        return at(1), at(2), at(4), at(8), (dec_ref[p, rows0:rows0 + SUBLANES, re],
                                             dec_ref[p, rows0:rows0 + SUBLANES, im])

    by_len = {}
    for start, length in seq_rows:
        by_len.setdefault(length, []).append(start)
    for length, starts in by_len.items():
        chains = [(st, p) for st in starts for p in range(pb)]
        tabs = [(tables(p, 0, 0), tables(p, 2, 2 * SUBLANES)) for p in range(pb)]

        def step(k8, carry, chains=chains, length=length, tabs=tabs):
            out = []
            for (st, p), (fr, fi, br, bi) in zip(chains, carry):
                rf = pl.multiple_of(st + k8 * SUBLANES, SUBLANES)
                rb = pl.multiple_of(st + length - SUBLANES - k8 * SUBLANES, SUBLANES)
                cols = [slice(p * w + c * sn, p * w + (c + 1) * sn) for c in range(4)]
                xfr, xfi, fr, fi = _tile_scan(s_ref[pl.ds(rf, SUBLANES), cols[0]], s_ref[pl.ds(rf, SUBLANES), cols[1]],
                                              fr, fi, *tabs[p][0], row, False)
                xbr, xbi, br, bi = _tile_scan(s_ref[pl.ds(rb, SUBLANES), cols[2]], s_ref[pl.ds(rb, SUBLANES), cols[3]],
                                              br, bi, *tabs[p][1], row, True)
                s_ref[pl.ds(rf, SUBLANES), cols[0]] = xfr
                s_ref[pl.ds(rf, SUBLANES), cols[1]] = xfi
                s_ref[pl.ds(rb, SUBLANES), cols[2]] = xbr
                s_ref[pl.ds(rb, SUBLANES), cols[3]] = xbi
                out.append((fr, fi, br, bi))
            return tuple(out)

        assert length % SUBLANES == 0 and all(st % SUBLANES == 0 for st in starts)
        zero = jnp.zeros((1, sn), F32)
        lax.fori_loop(0, length // SUBLANES, step, tuple((zero, zero, zero, zero) for _ in chains))

    for p in range(pb):
        def emit(c, carry, p=p):
            r0 = pl.multiple_of(c * rc, rc)
            u = u_ref[pl.ds(r0, rc), p * w:(p + 1) * w]
            x = s_ref[pl.ds(r0, rc), p * w:(p + 1) * w].astype(BF16)
            y_ref[pl.ds(r0, rc), p * w:(p + 1) * w] = (
                jnp.dot(u, wyu_ref[p], preferred_element_type=F32)
                + jnp.dot(x, wyx_ref[p], preferred_element_type=F32))
            return carry
        lax.fori_loop(0, n_rc, emit, 0)


def s5_core(u_blk, ops, seq_rows):
    ws, wyu, wyx, dec = ops
    rows, width = u_blk.shape
    hp, w, s4 = ws.shape
    pb = S5_PAIR_BLOCK if hp % S5_PAIR_BLOCK == 0 else 1
    rc = _tile(rows, S5_ROW_CHUNK, 8)
    kern = functools.partial(_s5_kernel, seq_rows=tuple(seq_rows), pb=pb, rc=rc)
    wspec = lambda a, b: pl.BlockSpec((pb, a, b), lambda j: (j, 0, 0))
    return pl.pallas_call(
        kern,
        name="s5_core",
        grid=(hp // pb,),
        in_specs=[pl.BlockSpec((rows, pb * w), lambda j: (0, j)),
                  wspec(w, s4), wspec(w, w), wspec(s4, w), wspec(dec.shape[1], s4)],
        out_specs=pl.BlockSpec((rows, pb * w), lambda j: (0, j)),
        out_shape=jax.ShapeDtypeStruct((rows, width), F32),
        scratch_shapes=[pltpu.VMEM((rows, pb * w), F32)],
        compiler_params=_params(("parallel",)),
    )(u_blk, ws, wyu, wyx, dec)


def s5_layer(x, seqs, mix_g, in_w, ssm, d_skip, glu_w, glu_b, out_w):
    e = glu_w.shape[0]
    p = ssm[3].shape[-1]
    h = rmsnorm(x, mix_g, BF16)
    uz = matmul(h, in_w.astype(BF16), out_dtype=F32)
    u_blk = s5_pack(uz, e, p)
    y_blk = s5_core(u_blk, _s5_operators(*ssm), [(s // S5_TB, l // S5_TB) for s, l in seqs])
    y = s5_unpack(y_blk, uz, d_skip, e, p)
    gated = matmul(y, glu_w.astype(BF16), out_dtype=BF16, mode="glu",
                   extra=(glu_b.reshape(1, e).astype(F32), y, uz),
                   extra_specs=(_row_spec, _tile_spec(), _tile_spec(e)), tm=512, tn=1024)
    return matmul(gated, out_w.astype(BF16), out_dtype=F32, mode="residual",
                  extra=(x,), extra_specs=(_tile_spec(),), tn=512)


def _conv_qkv_kernel(flags_ref, xm_ref, prev_ref, next_ref, cw_ref, cb_ref, wq_ref, wk_ref, wkt_ref, wv_ref,
                     xc_ref, q_ref, k_ref, kt_ref, v_ref, pad_ref, *, ksize, k_scale):
    i = pl.program_id(0)
    tm = xm_ref.shape[0]
    half = ksize // 2
    xm = xm_ref[...]
    pad_ref[0:8, :] = prev_ref[...] * flags_ref[0, i].astype(F32)
    pad_ref[8:8 + tm, :] = xm
    pad_ref[8 + tm:16 + tm, :] = next_ref[...] * flags_ref[1, i].astype(F32)
    acc = cb_ref[...] + jnp.zeros_like(xm)
    for k in range(ksize):
        acc = acc + pad_ref[8 + k - half:8 + k - half + tm, :] * cw_ref[k:k + 1, :]
    xc = _silu(acc)
    xc_ref[...] = xc
    xcb = xc.astype(BF16)
    q_ref[...] = jnp.dot(xcb, wq_ref[0], preferred_element_type=F32).astype(q_ref.dtype)
    k_ref[...] = (jnp.dot(xcb, wk_ref[0], preferred_element_type=F32) * k_scale).astype(k_ref.dtype)
    kt = lax.dot_general(wkt_ref[0], xcb, (((1,), (1,)), ((), ())), preferred_element_type=F32)
    kt_ref[...] = (kt * k_scale).astype(kt_ref.dtype)
    v_ref[...] = jnp.dot(xm.astype(BF16), wv_ref[0], preferred_element_type=F32).astype(v_ref.dtype)


def _dense_block_diag(wb, width):
    nb, b, _ = wb.shape
    per = width // b
    tiles = wb.reshape(nb // per, per, b, b)
    eye = jnp.eye(per, dtype=wb.dtype)
    return jnp.einsum("tncd,nm->tncmd", tiles, eye).reshape(nb // per, width, width)


def conv_qkv(proj, seqs, conv_w, conv_b, wq, wk, wv, e, dk):
    t = proj.shape[0]
    ksize = conv_w.shape[0]
    cw = _tile(e, QKV_DENSE, LANES)
    lmin = functools.reduce(math.gcd, [l for _, l in seqs])
    tm = _tile(lmin, 512, LANES)
    nt = t // tm
    starts = {s for s, _ in seqs}
    ends = {s + l for s, l in seqs}
    flags = jnp.array([[0 if i * tm in starts else 1 for i in range(nt)],
                       [0 if (i + 1) * tm in ends else 1 for i in range(nt)]], jnp.int32)
    wqd = _dense_block_diag(wq, cw).astype(BF16)
    wkd = _dense_block_diag(wk, cw).astype(BF16)
    wktd = jnp.swapaxes(wkd, 1, 2)
    wvd = _dense_block_diag(wv, cw).astype(BF16)
    r8 = tm // 8
    last8 = t // 8 - 1
    kern = functools.partial(_conv_qkv_kernel, ksize=ksize, k_scale=float(dk) ** -0.5)
    wspec = pl.BlockSpec((1, cw, cw), lambda i, c, f: (c, 0, 0))
    tile = pl.BlockSpec((tm, cw), lambda i, c, f: (i, c))
    outs = pl.pallas_call(
        kern,
        name="conv_qkv",
        grid_spec=pltpu.PrefetchScalarGridSpec(
            num_scalar_prefetch=1,
            grid=(nt, e // cw),
            in_specs=[tile,
                      pl.BlockSpec((8, cw), lambda i, c, f: (jnp.maximum(i * r8 - 1, 0), c)),
                      pl.BlockSpec((8, cw), lambda i, c, f: (jnp.minimum((i + 1) * r8, last8), c)),
                      pl.BlockSpec((ksize, cw), lambda i, c, f: (0, c)),
                      pl.BlockSpec((1, cw), lambda i, c, f: (0, c)),
                      wspec, wspec, wspec, wspec],
            out_specs=[tile, tile, tile, pl.BlockSpec((cw, tm), lambda i, c, f: (c, i)), tile],
            scratch_shapes=[pltpu.VMEM((tm + 16, cw), F32)]),
        out_shape=[jax.ShapeDtypeStruct((t, e), F32), jax.ShapeDtypeStruct((t, e), BF16),
                   jax.ShapeDtypeStruct((t, e), BF16), jax.ShapeDtypeStruct((e, t), BF16),
                   jax.ShapeDtypeStruct((t, e), BF16)],
        compiler_params=_params(("parallel", "parallel")),
    )(flags, proj, proj, proj, conv_w.astype(F32), conv_b.reshape(1, e).astype(F32), wqd, wkd, wktd, wvd)
    return outs


def _log_sigmoid(x):
    return jnp.minimum(x, 0.0) - jnp.log(1.0 + jnp.exp(-jnp.abs(x)))


def _mlstm_kernel(reset_ref, q_ref, k_ref, kt_ref, v_ref, ic_ref, fc_ref, ir_ref, fr_ref, h_ref,
                  c_ref, n_ref, m_ref, *, hb, dk):
    d = pl.program_id(0)
    c = pl.program_id(2)
    ch = q_ref.shape[0]
    hi = lax.Precision.HIGHEST

    @pl.when(reset_ref[d, c] == 1)
    def _():
        c_ref[...] = jnp.zeros_like(c_ref)
        n_ref[...] = jnp.zeros_like(n_ref)
        m_ref[...] = jnp.zeros_like(m_ref)

    tt = lax.broadcasted_iota(jnp.int32, (ch, ch), 0)
    ss = lax.broadcasted_iota(jnp.int32, (ch, ch), 1)
    sign = 1 - 2 * d
    allowed = (ss - tt) * sign <= 0
    tri = jnp.where(allowed, 1.0, 0.0).astype(F32)
    tri_t = jnp.where((tt - ss) * sign <= 0, 1.0, 0.0).astype(F32)

    lf_col = _log_sigmoid(fc_ref[0, 0])
    lf_row = _log_sigmoid(fr_ref[0, 0])
    bcum_col = jnp.dot(tri, lf_col, preferred_element_type=F32, precision=hi)
    bcum_row = jnp.dot(lf_row, tri_t, preferred_element_type=F32, precision=hi)
    i_col = ic_ref[0, 0]
    i_row = ir_ref[0, 0]
    blast_all = jnp.sum(lf_row, axis=-1, keepdims=True)

    for j in range(hb):
        lanes = slice(j * dk, (j + 1) * dk)
        bc = bcum_col[:, j:j + 1]
        ic = i_col[:, j:j + 1]
        br = bcum_row[j:j + 1, :]
        ir = i_row[j:j + 1, :]
        m_prev = m_ref[j, 0:1, 0:1]
        b_last = blast_all[j:j + 1, :]

        dmat = jnp.where(allowed, bc - br + ir, NEG_BIG)
        inter = bc + m_prev
        m_t = jnp.maximum(inter, jnp.max(dmat, axis=-1, keepdims=True))
        w_inter = jnp.exp(inter - m_t)
        qj = q_ref[:, lanes]
        s = jnp.dot(qj, kt_ref[lanes, :], preferred_element_type=F32) * jnp.exp(dmat - m_t)
        cmat = c_ref[j]
        nrow = n_ref[j]
        num = (w_inter * jnp.dot(qj, cmat.astype(BF16), preferred_element_type=F32)
               + jnp.dot(s.astype(BF16), v_ref[:, lanes], preferred_element_type=F32))
        den = (w_inter * jnp.sum(qj.astype(F32) * nrow, axis=-1, keepdims=True)
               + jnp.sum(s, axis=-1, keepdims=True))
        h_ref[0, :, lanes] = num / jnp.maximum(jnp.abs(den), jnp.exp(-m_t))

        g_row = b_last - br + ir
        m_new = jnp.maximum(b_last + m_prev, jnp.max(g_row, axis=-1, keepdims=True))
        decay = jnp.exp(b_last + m_prev - m_new)
        wg_row = jnp.exp(g_row - m_new)
        wg_col = jnp.exp(b_last - bc + ic - m_new)
        kts = (kt_ref[lanes, :].astype(F32) * wg_row).astype(BF16)
        c_ref[j] = decay * cmat + jnp.dot(kts, v_ref[:, lanes], preferred_element_type=F32)
        n_ref[j] = decay * nrow + jnp.sum(k_ref[:, lanes].astype(F32) * wg_col, axis=0, keepdims=True)
        m_ref[j] = jnp.broadcast_to(m_new, m_ref.shape[1:])


def mlstm_core(q, k, kt, v, gates, seqs, nh):
    t, e = q.shape
    dk = e // nh
    hb = MLSTM_HEAD_BLOCK if nh % MLSTM_HEAD_BLOCK == 0 else nh
    nhb = nh // hb
    lmin = functools.reduce(math.gcd, [l for _, l in seqs])
    ch = _tile(lmin, MLSTM_CHUNK, LANES)
    nc = t // ch
    starts = {s for s, _ in seqs}
    ends = {s + l for s, l in seqs}
    reset = jnp.array([[1 if c * ch in starts else 0 for c in range(nc)],
                       [1 if (nc - c) * ch in ends else 0 for c in range(nc)]], jnp.int32)
    g4 = gates[:, :4 * nh].reshape(t, 4, nhb, hb)
    g_row = jnp.transpose(g4, (1, 2, 3, 0))
    g_col = jnp.pad(jnp.transpose(g4, (1, 2, 0, 3)), ((0, 0), (0, 0), (0, 0), (0, LANES - hb)))

    cidx = lambda d, c: c + d * (nc - 1 - 2 * c)
    tile = pl.BlockSpec((ch, hb * dk), lambda d, h, c, r: (cidx(d, c), h))
    colspec = lambda off: pl.BlockSpec((1, 1, ch, LANES), lambda d, h, c, r: (2 * d + off, h, cidx(d, c), 0))
    rowspec = lambda off: pl.BlockSpec((1, 1, hb, ch), lambda d, h, c, r: (2 * d + off, h, 0, cidx(d, c)))
    kern = functools.partial(_mlstm_kernel, hb=hb, dk=dk)
    return pl.pallas_call(
        kern,
        name="mlstm_core",
        grid_spec=pltpu.PrefetchScalarGridSpec(
            num_scalar_prefetch=1,
            grid=(2, nhb, nc),
            in_specs=[tile, tile,
                      pl.BlockSpec((hb * dk, ch), lambda d, h, c, r: (h, cidx(d, c))),
                      tile, colspec(0), colspec(1), rowspec(0), rowspec(1)],
            out_specs=pl.BlockSpec((1, ch, hb * dk), lambda d, h, c, r: (d, cidx(d, c), h)),
            scratch_shapes=[pltpu.VMEM((hb, dk, dk), F32), pltpu.VMEM((hb, 1, dk), F32),
                            pltpu.VMEM((hb, 8, LANES), F32)]),
        out_shape=jax.ShapeDtypeStruct((2, t, e), F32),
        compiler_params=_params(("parallel", "parallel", "arbitrary")),
    )(reset, q, k, kt, v, g_col, g_col, g_row, g_row)


def _mlstm_post_kernel(hf_ref, hb_ref, o_ref, z_ref, xc_ref, hn_ref, sk_ref, out_ref):
    hs = hf_ref[0] + hb_ref[0]
    hs = hs * lax.rsqrt(jnp.mean(hs * hs, axis=-1, keepdims=True) + EPS)
    hs = hs * hn_ref[...]
    hs = _sigmoid(o_ref[...]) * hs + sk_ref[...] * xc_ref[...]
    out_ref[...] = (hs * _silu(z_ref[...])).astype(out_ref.dtype)


def mlstm_post(hdir, proj, xc, head_norm, skip, e, nh):
    t = xc.shape[0]
    dk = e // nh
    tm = _tile(t, 1024, 8)
    hspec = lambda d: pl.BlockSpec((1, tm, dk), lambda i, h: (d, i, h))
    pspec = lambda off: pl.BlockSpec((tm, dk), lambda i, h: (i, h + off * nh))
    vec = pl.BlockSpec((1, dk), lambda i, h: (0, h))
    return pl.pallas_call(
        _mlstm_post_kernel,
        name="mlstm_post",
        grid=(t // tm, nh),
        in_specs=[hspec(0), hspec(1), pspec(2), pspec(1), pspec(0), vec, vec],
        out_specs=pl.BlockSpec((tm, dk), lambda i, h: (i, h)),
        out_shape=jax.ShapeDtypeStruct((t, e), BF16),
        compiler_params=_params(("parallel", "parallel")),
    )(hdir, hdir, proj, proj, xc, head_norm.reshape(1, e).astype(F32), skip.reshape(1, e).astype(F32))


def mlstm_layer(x, seqs, mix_g, in_w, gate_b, conv_w, conv_b, wq, wk, wv, skip, head_norm, out_w):
    e = out_w.shape[0]
    nh = gate_b.shape[0] // 4
    h = rmsnorm(x, mix_g, BF16)
    proj = matmul(h, in_w[:, :3 * e].astype(BF16), out_dtype=F32)
    gw = jnp.pad(in_w[:, 3 * e:], ((0, 0), (0, LANES - 4 * nh))).astype(BF16)
    gb = jnp.pad(gate_b, (0, LANES - 4 * nh)).reshape(1, LANES).astype(F32)
    gates = matmul(h, gw, out_dtype=F32, mode="bias", extra=(gb,), extra_specs=(_row_spec,))
    xc, q, k, kt, v = conv_qkv(proj, seqs, conv_w, conv_b, wq, wk, wv, e, e // nh)
    hdir = mlstm_core(q, k, kt, v, gates, seqs, nh)
    mixed = mlstm_post(hdir, proj, xc, head_norm, skip, e, nh)
    return matmul(mixed, out_w.astype(BF16), out_dtype=F32, mode="residual",
                  extra=(x,), extra_specs=(_tile_spec(),), tn=512)


def _xattn_kernel(seq_ref, q_ref, k_ref, v_ref, o_ref, *, scale):
    del seq_ref
    s = lax.dot_general(q_ref[...], k_ref[0], (((1,), (1,)), ((), ())), preferred_element_type=F32) * scale
    s = s - jnp.max(s, axis=-1, keepdims=True)
    p = jnp.exp(s)
    p = p / jnp.sum(p, axis=-1, keepdims=True)
    o_ref[...] = jnp.dot(p.astype(BF16), v_ref[0], preferred_element_type=F32).astype(o_ref.dtype)


def xattn_core(q, kmem, vmem, seqs):
    t, d = q.shape
    hd = d // XA_HEADS
    lmin = functools.reduce(math.gcd, [l for _, l in seqs])
    tm = _tile(lmin, 512, 8)
    seq_of = jnp.array([max(j for j, (s, _) in enumerate(seqs) if s <= i * tm) for i in range(t // tm)],
                       jnp.int32)
    m = kmem.shape[1]
    kern = functools.partial(_xattn_kernel, scale=float(hd) ** -0.5)
    mem_spec = pl.BlockSpec((1, m, hd), lambda i, h, sq: (sq[i], 0, h))
    return pl.pallas_call(
        kern,
        name="xattn_core",
        grid_spec=pltpu.PrefetchScalarGridSpec(
            num_scalar_prefetch=1,
            grid=(t // tm, XA_HEADS),
            in_specs=[pl.BlockSpec((tm, hd), lambda i, h, sq: (i, h)), mem_spec, mem_spec],
            out_specs=pl.BlockSpec((tm, hd), lambda i, h, sq: (i, h))),
        out_shape=jax.ShapeDtypeStruct((t, d), BF16),
        compiler_params=_params(("parallel", "parallel")),
    )(seq_of, q, kmem, vmem)


def xattn_layer(x, mem, seqs, xa_g, mem_g, wq, wk, wv, wo):
    ns, m, d = mem.shape
    hq = rmsnorm(x, xa_g, BF16)
    mem_n = rmsnorm(mem.reshape(ns * m, d), mem_g, BF16)
    q = matmul(hq, wq.astype(BF16), out_dtype=BF16)
    kmem = matmul(mem_n, wk.astype(BF16), out_dtype=BF16).reshape(ns, m, d)
    vmem = matmul(mem_n, wv.astype(BF16), out_dtype=BF16).reshape(ns, m, d)
    o = xattn_core(q, kmem, vmem, seqs)
    return matmul(o, wo.astype(BF16), out_dtype=F32, mode="residual", extra=(x,), extra_specs=(_tile_spec(),))


def kernel(x_prompt, x_sample, mem_prompt, mem_sample, mix_norm, a_in_w, s5_a_re, s5_a_im, s5_log_step,
           s5_b_re, s5_b_im, s5_c_re, s5_c_im, s5_d, a_glu_w, a_glu_b, a_out_w, b_in_w, b_gate_bias,
           b_conv_w, b_conv_b, b_wq, b_wk, b_wv, b_skip, b_head_norm, b_out_w, xa_norm, mem_norm,
           xa_wq, xa_wk, xa_wv, xa_wo, final_norm):
    bp, lp, d = x_prompt.shape
    bs, ls, _ = x_sample.shape
    depth = mix_norm.shape[0]
    seqs = [(i * lp, lp) for i in range(bp)] + [(bp * lp + i * ls, ls) for i in range(bs)]
    x = jnp.concatenate([x_prompt.reshape(bp * lp, d), x_sample.reshape(bs * ls, d)], axis=0)
    mem = jnp.concatenate([mem_prompt, mem_sample], axis=0)
    for i in range(depth):
        j = i // 2
        if i % 2 == 0:
            ssm = (s5_a_re[j], s5_a_im[j], s5_log_step[j], s5_b_re[j], s5_b_im[j], s5_c_re[j], s5_c_im[j])
            x = s5_layer(x, seqs, mix_norm[i], a_in_w[j], ssm, s5_d[j], a_glu_w[j], a_glu_b[j], a_out_w[j])
        else:
            x = mlstm_layer(x, seqs, mix_norm[i], b_in_w[j], b_gate_bias[j], b_conv_w[j], b_conv_b[j],
                            b_wq[j], b_wk[j], b_wv[j], b_skip[j], b_head_norm[j], b_out_w[j])
        x = xattn_layer(x, mem, seqs, xa_norm[i], mem_norm[i], xa_wq[i], xa_wk[i], xa_wv[i], xa_wo[i])
    y = rmsnorm(x, final_norm, F32)
    return (y[:bp * lp].reshape(bp, lp, d), y[bp * lp:].reshape(bs, ls, d))
```

```python
import functools
import math

import jax
import jax.numpy as jnp
from jax import lax
from jax.experimental import pallas as pl
from jax.experimental.pallas import tpu as pltpu

F32 = jnp.float32
BF16 = jnp.bfloat16

XA_HEADS = 4
EPS = 1e-6
S5_TB = 16
S5_PAIR_BLOCK = 2
S5_ROW_CHUNK = 256
S5_RELAYOUT_ROWS = 128
S5_RELAYOUT_CHUNK = 32
MLSTM_CHUNK = 256
MLSTM_HEAD_BLOCK = 8
QKV_DENSE = 256
LANES = 128
MXU_COLS = 256
SUBLANES = 8
V7X_VMEM_LIMIT = 56 * 1024 * 1024
NEG_BIG = -1e30


def _tile(n, pref, mult):
    best = None
    for t in range(mult, min(n, pref) + 1, mult):
        if n % t == 0:
            best = t
    return n if best is None else best


def _params(sem):
    return pltpu.CompilerParams(dimension_semantics=sem, vmem_limit_bytes=V7X_VMEM_LIMIT)


def _rmsnorm_kernel(x_ref, g_ref, o_ref):
    x = x_ref[...].astype(F32)
    ms = jnp.mean(x * x, axis=-1, keepdims=True)
    o_ref[...] = (x * lax.rsqrt(ms + EPS) * g_ref[...]).astype(o_ref.dtype)


def rmsnorm(x, g, out_dtype):
    t, d = x.shape
    tm = _tile(t, 512, 8)
    return pl.pallas_call(
        _rmsnorm_kernel,
        name="rmsnorm",
        grid=(t // tm,),
        in_specs=[pl.BlockSpec((tm, d), lambda i: (i, 0)),
                  pl.BlockSpec((1, d), lambda i: (0, 0))],
        out_specs=pl.BlockSpec((tm, d), lambda i: (i, 0)),
        out_shape=jax.ShapeDtypeStruct((t, d), out_dtype),
        compiler_params=_params(("parallel",)),
    )(x, g.reshape(1, d).astype(F32))


def _sigmoid(x):
    return 1.0 / (1.0 + jnp.exp(-x))


def _silu(x):
    return x * _sigmoid(x)


def _mm_epilogue(mode, acc, extra, cols):
    if mode == "plain":
        return acc
    if mode == "bias":
        return acc + extra[0][:, cols]
    if mode == "residual":
        return extra[0][:, cols] + acc
    if mode == "glu":
        b_ref, y_ref, z_ref = extra
        y = y_ref[:, cols].astype(F32)
        return y * _sigmoid(acc + b_ref[:, cols]) * _silu(z_ref[:, cols])
    raise ValueError(mode)


def _mm_kernel(*refs, nk, mode, n_extra):
    a_ref, w_ref = refs[0], refs[1]
    extra = refs[2:2 + n_extra]
    o_ref = refs[2 + n_extra]
    acc_ref = refs[3 + n_extra] if nk > 1 else None
    tn = o_ref.shape[1]
    cw = MXU_COLS if tn % MXU_COLS == 0 else tn

    def finish():
        for c0 in range(0, tn, cw):
            cols = slice(c0, c0 + cw)
            acc = jnp.dot(a_ref[...], w_ref[:, cols], preferred_element_type=F32)
            if acc_ref is not None:
                acc = acc_ref[:, cols] + acc
            o_ref[:, cols] = _mm_epilogue(mode, acc, extra, cols).astype(o_ref.dtype)

    if nk == 1:
        finish()
        return
    k = pl.program_id(2)

    @pl.when(k == 0)
    def _():
        acc_ref[...] = jnp.dot(a_ref[...], w_ref[...], preferred_element_type=F32)

    @pl.when(jnp.logical_and(k > 0, k < nk - 1))
    def _():
        acc_ref[...] += jnp.dot(a_ref[...], w_ref[...], preferred_element_type=F32)

    pl.when(k == nk - 1)(finish)


def matmul(a, w, *, out_dtype, mode="plain", extra=(), extra_specs=(), tm=1024, tn=1024, tk=4096):
    m, kd = a.shape
    n = w.shape[1]
    tm, tn, tk = _tile(m, tm, 8), _tile(n, tn, LANES), _tile(kd, tk, LANES)
    nk = kd // tk
    kern = functools.partial(_mm_kernel, nk=nk, mode=mode, n_extra=len(extra))
    in_specs = [pl.BlockSpec((tm, tk), lambda i, j, k: (i, k)),
                pl.BlockSpec((tk, tn), lambda i, j, k: (k, j))]
    in_specs += [mk(tm, tn) for mk in extra_specs]
    return pl.pallas_call(
        kern,
        name="matmul_" + mode,
        grid=(m // tm, n // tn, nk),
        in_specs=in_specs,
        out_specs=pl.BlockSpec((tm, tn), lambda i, j, k: (i, j)),
        out_shape=jax.ShapeDtypeStruct((m, n), out_dtype),
        scratch_shapes=[pltpu.VMEM((tm, tn), F32)] if nk > 1 else [],
        compiler_params=_params(("parallel", "parallel", "arbitrary")),
    )(a, w, *extra)


def _row_spec(tm, tn):
    return pl.BlockSpec((1, tn), lambda i, j, k: (0, j))


def _tile_spec(col_off=0):
    def mk(tm, tn):
        off = col_off // tn
        return pl.BlockSpec((tm, tn), lambda i, j, k: (i, j + off))
    return mk


def _s5_operators(a_re, a_im, log_step, b_re, b_im, c_re, c_im):
    hi = lax.Precision.HIGHEST
    tb = S5_TB
    _, g, n = a_re.shape
    p = b_re.shape[-1]
    hp, m = g // 2, tb * p
    lam_re = jnp.minimum(a_re.astype(F32), -1e-4)
    lam_im = a_im.astype(F32)
    dt = jnp.exp(log_step.astype(F32))[..., None]
    mag = jnp.exp(lam_re * dt)
    ab_re = mag * jnp.cos(lam_im * dt)
    ab_im = mag * jnp.sin(lam_im * dt)
    den = lam_re * lam_re + lam_im * lam_im
    nr, ni = ab_re - 1.0, ab_im
    f_re = (nr * lam_re + ni * lam_im) / den
    f_im = (ni * lam_re - nr * lam_im) / den
    b_re, b_im = b_re.astype(F32), b_im.astype(F32)
    bb_re = f_re[..., None] * b_re - f_im[..., None] * b_im
    bb_im = f_re[..., None] * b_im + f_im[..., None] * b_re
    c_re, c_im = c_re.astype(F32), c_im.astype(F32)

    def powers(steps):
        pmag = jnp.exp((lam_re * dt)[..., None] * steps)
        pang = (lam_im * dt)[..., None] * steps
        return pmag * jnp.cos(pang), pmag * jnp.sin(pang)

    pw_re, pw_im = powers(jnp.arange(tb + 1, dtype=F32))
    rw_re, rw_im = powers(jnp.arange(SUBLANES + 1, dtype=F32) * tb)

    def c_times_powers(cr, ci, pr, pi):
        ct_re = jnp.tile(jnp.swapaxes(cr, 1, 2), (1, 1, tb + 1))
        ct_im = jnp.tile(jnp.swapaxes(ci, 1, 2), (1, 1, tb + 1))
        rp_re, rp_im = jnp.repeat(pr, p, axis=-1), jnp.repeat(pi, p, axis=-1)
        return ct_re * rp_re - ct_im * rp_im, ct_re * rp_im + ct_im * rp_re

    ef_re, ef_im = c_times_powers(c_re[0], c_im[0], pw_re[0], pw_im[0])
    eb_re, eb_im = c_times_powers(c_re[1], c_im[1], pw_re[1][..., ::-1], pw_im[1][..., ::-1])

    def lag_rows(br, bi, er, ei):
        return (jnp.einsum("gnq,gnm->gqm", br, er, precision=hi)
                - jnp.einsum("gnq,gnm->gqm", bi, ei, precision=hi))

    kt_f = lag_rows(bb_re[0], bb_im[0], ef_re[..., :m], ef_im[..., :m])
    kt_b = lag_rows(bb_re[1], bb_im[1], eb_re[..., p:], eb_im[..., p:])
    zpad = jnp.zeros_like(kt_f)
    kf_pad = jnp.concatenate([zpad, kt_f], axis=-1)
    kb_pad = jnp.concatenate([kt_b, zpad], axis=-1)
    kloc = jnp.stack([kf_pad[..., m - p * s:2 * m - p * s] + kb_pad[..., (tb - 1 - s) * p:(tb - 1 - s) * p + m]
                      for s in range(tb)], axis=1).reshape(g, m, m)

    cf_re, cf_im = ef_re[..., p:], -ef_im[..., p:]
    cb_re, cb_im = eb_re[..., :m], -eb_im[..., :m]

    def state_in(pr, pi, br, bi):
        pr_t, pi_t = jnp.swapaxes(pr, 1, 2)[:, :, None, :], jnp.swapaxes(pi, 1, 2)[:, :, None, :]
        br_t, bi_t = jnp.swapaxes(br, 1, 2)[:, None, :, :], jnp.swapaxes(bi, 1, 2)[:, None, :, :]
        return ((pr_t * br_t - pi_t * bi_t).reshape(g, m, n), (pr_t * bi_t + pi_t * br_t).reshape(g, m, n))

    sf_re, sf_im = state_in(pw_re[0][..., tb - 1::-1], pw_im[0][..., tb - 1::-1], bb_re[0], bb_im[0])
    sb_re, sb_im = state_in(pw_re[1][..., :tb], pw_im[1][..., :tb], bb_re[1], bb_im[1])

    def block_diag2(a):
        a = a.reshape((hp, 2) + a.shape[1:])
        z = jnp.zeros_like(a[:, 0])
        return jnp.concatenate([jnp.concatenate([a[:, 0], z], axis=-1),
                                jnp.concatenate([z, a[:, 1]], axis=-1)], axis=1)

    wyu = block_diag2(kloc)
    ws = jnp.concatenate([block_diag2(c) for c in (sf_re, sf_im, sb_re, sb_im)], axis=-1)
    wyx = jnp.concatenate([block_diag2(c) for c in (cf_re, cf_im, cb_re, cb_im)], axis=1)

    def pair_rows(a):
        return jnp.transpose(a.reshape(hp, 2, n, a.shape[-1]), (0, 3, 1, 2)).reshape(hp, a.shape[-1], 2 * n)

    asc = jnp.concatenate([pair_rows(c) for c in (rw_re[0], rw_im[0], rw_re[1], rw_im[1])], axis=-1)
    dec = jnp.concatenate([asc, jnp.zeros((hp, 2 * SUBLANES - asc.shape[1], 8 * n), F32),
                           asc[:, SUBLANES - 1::-1]], axis=1)
    return ws.astype(BF16), wyu.astype(BF16), wyx.astype(BF16), dec


def _piece_transpose(vs, slot, p):
    vs = list(vs)
    d = len(vs) // 2
    while d >= 1:
        upper = (slot & d) != 0
        for i in range(len(vs)):
            if i & d == 0:
                a, b = vs[i], vs[i + d]
                vs[i] = jnp.where(upper, pltpu.roll(b, d * p, axis=1), a)
                vs[i + d] = jnp.where(upper, b, pltpu.roll(a, LANES - d * p, axis=1))
        d //= 2
    return vs


def _s5_pack_kernel(u_ref, o_ref, *, tb, p):
    gpv = LANES // p
    halves = tb // gpv
    rs = S5_RELAYOUT_CHUNK
    slot = lax.broadcasted_iota(jnp.int32, (rs, LANES), 1) // p
    for r0 in range(0, o_ref.shape[0], rs):
        for c in range(halves):
            xs = [u_ref[pl.ds(r0 * tb + c * gpv + j, rs, stride=tb), :] for j in range(gpv)]
            for g in range(gpv):
                acc = None
                for j in range(gpv):
                    r = xs[j] if j == g else pltpu.roll(xs[j], (p * (j - g)) % LANES, axis=1)
                    acc = r if acc is None else jnp.where(slot == j, r, acc)
                lane0 = (g * halves + c) * LANES
                o_ref[r0:r0 + rs, lane0:lane0 + LANES] = acc.astype(o_ref.dtype)


def _s5_unpack_kernel(y_ref, u_ref, d_ref, o_ref, nat_ref, *, tb, p):
    gpv = LANES // p
    halves = tb // gpv
    rs = S5_RELAYOUT_CHUNK
    slot = lax.broadcasted_iota(jnp.int32, (rs, LANES), 1) // p
    for r0 in range(0, y_ref.shape[0], rs):
        for c in range(halves):
            chunks = [y_ref[r0:r0 + rs, (g * halves + c) * LANES:(g * halves + c + 1) * LANES] for g in range(gpv)]
            for j, v in enumerate(_piece_transpose(chunks, slot, p)):
                rows = pl.ds(r0 * tb + c * gpv + j, rs, stride=tb)
                nat_ref[rows, :] = jax.nn.gelu(v + d_ref[...] * u_ref[rows, :])
    o_ref[...] = nat_ref[...].astype(o_ref.dtype)


def _relayout_blocks(t, e, tb):
    rb = _tile(t // tb, S5_RELAYOUT_ROWS, S5_RELAYOUT_CHUNK)
    assert rb % S5_RELAYOUT_CHUNK == 0 and e % LANES == 0
    return rb, LANES


def s5_pack(uz, e, p):
    t, tb = uz.shape[0], S5_TB
    rb, w = _relayout_blocks(t, e, tb)
    return pl.pallas_call(
        functools.partial(_s5_pack_kernel, tb=tb, p=p),
        name="s5_pack",
        grid=(t // tb // rb, e // w),
        in_specs=[pl.BlockSpec((rb * tb, w), lambda i, j: (i, j))],
        out_specs=pl.BlockSpec((rb, w * tb), lambda i, j: (i, j)),
        out_shape=jax.ShapeDtypeStruct((t // tb, e * tb), BF16),
        compiler_params=_params(("parallel", "parallel")),
    )(uz)


def s5_unpack(y_blk, uz, d_skip, e, p):
    t, tb = uz.shape[0], S5_TB
    rb, w = _relayout_blocks(t, e, tb)
    return pl.pallas_call(
        functools.partial(_s5_unpack_kernel, tb=tb, p=p),
        name="s5_unpack",
        grid=(t // tb // rb, e // w),
        in_specs=[pl.BlockSpec((rb, w * tb), lambda i, j: (i, j)),
                  pl.BlockSpec((rb * tb, w), lambda i, j: (i, j)),
                  pl.BlockSpec((1, w), lambda i, j: (0, j))],
        out_specs=pl.BlockSpec((rb * tb, w), lambda i, j: (i, j)),
        out_shape=jax.ShapeDtypeStruct((t, e), BF16),
        scratch_shapes=[pltpu.VMEM((rb * tb, w), F32)],
        compiler_params=_params(("parallel", "parallel")),
    )(y_blk, uz, d_skip.reshape(1, e).astype(F32))


def _cmul(ar, ai, br, bi):
    return ar * br - ai * bi, ar * bi + ai * br


def _tile_scan(sr, si, xr, xi, d1, d2, d4, d8, pk, row, reverse):
    def shifted(v, dist):
        if reverse:
            return jnp.where(row < SUBLANES - dist, pltpu.roll(v, SUBLANES - dist, axis=0), 0.0)
        return jnp.where(row >= dist, pltpu.roll(v, dist, axis=0), 0.0)

    ir, ii = sr, si
    for dist, (dr, di) in ((1, d1), (2, d2), (4, d4)):
        mr, mi = _cmul(dr, di, shifted(ir, dist), shifted(ii, dist))
        ir, ii = ir + mr, ii + mi
    cr, ci = _cmul(pk[0], pk[1], xr, xi)
    orr, oi = _cmul(d8[0], d8[1], xr, xi)
    last = 0 if reverse else SUBLANES - 1
    return (shifted(ir, 1) + cr, shifted(ii, 1) + ci,
            ir[last:last + 1] + orr, ii[last:last + 1] + oi)


def _s5_kernel(u_ref, ws_ref, wyu_ref, wyx_ref, dec_ref, y_ref, s_ref, *, seq_rows, pb, rc):
    rows = u_ref.shape[0]
    w = ws_ref.shape[1]
    sn = ws_ref.shape[2] // 4
    n_rc = rows // rc

    for p in range(pb):
        def fill(c, carry, p=p):
            r0 = pl.multiple_of(c * rc, rc)
            u = u_ref[pl.ds(r0, rc), p * w:(p + 1) * w]
            s_ref[pl.ds(r0, rc), p * w:(p + 1) * w] = jnp.dot(u, ws_ref[p], preferred_element_type=F32)
            return carry
        lax.fori_loop(0, n_rc, fill, 0)

    row = lax.broadcasted_iota(jnp.int32, (SUBLANES, sn), 0)

    def tables(p, c_re, rows0):
        re, im = slice(c_re * sn, (c_re + 1) * sn), slice((c_re + 1) * sn, (c_re + 2) * sn)
        at = lambda k: (dec_ref[p, k:k + 1, re], dec_ref[p, k:k + 1, im])
        return at(1), at(2), at(4), at(8), (dec_ref[p, rows0:rows0 + SUBLANES, re],
                                             dec_ref[p, rows0:rows0 + SUBLANES, im])

    by_len = {}
    for start, length in seq_rows:
        by_len.setdefault(length, []).append(start)
    for length, starts in by_len.items():
        chains = [(st, p) for st in starts for p in range(pb)]
        tabs = [(tables(p, 0, 0), tables(p, 2, 2 * SUBLANES)) for p in range(pb)]

        def step(k8, carry, chains=chains, length=length, tabs=tabs):
            out = []
            for (st, p), (fr, fi, br, bi) in zip(chains, carry):
                rf = pl.multiple_of(st + k8 * SUBLANES, SUBLANES)
                rb = pl.multiple_of(st + length - SUBLANES - k8 * SUBLANES, SUBLANES)
                cols = [slice(p * w + c * sn, p * w + (c + 1) * sn) for c in range(4)]
                xfr, xfi, fr, fi = _tile_scan(s_ref[pl.ds(rf, SUBLANES), cols[0]], s_ref[pl.ds(rf, SUBLANES), cols[1]],
                                              fr, fi, *tabs[p][0], row, False)
                xbr, xbi, br, bi = _tile_scan(s_ref[pl.ds(rb, SUBLANES), cols[2]], s_ref[pl.ds(rb, SUBLANES), cols[3]],
                                              br, bi, *tabs[p][1], row, True)
                s_ref[pl.ds(rf, SUBLANES), cols[0]] = xfr
                s_ref[pl.ds(rf, SUBLANES), cols[1]] = xfi
                s_ref[pl.ds(rb, SUBLANES), cols[2]] = xbr
                s_ref[pl.ds(rb, SUBLANES), cols[3]] = xbi
                out.append((fr, fi, br, bi))
            return tuple(out)

        assert length % SUBLANES == 0 and all(st % SUBLANES == 0 for st in starts)
        zero = jnp.zeros((1, sn), F32)
        lax.fori_loop(0, length // SUBLANES, step, tuple((zero, zero, zero, zero) for _ in chains))

    for p in range(pb):
        def emit(c, carry, p=p):
            r0 = pl.multiple_of(c * rc, rc)
            u = u_ref[pl.ds(r0, rc), p * w:(p + 1) * w]
            x = s_ref[pl.ds(r0, rc), p * w:(p + 1) * w].astype(BF16)
            y_ref[pl.ds(r0, rc), p * w:(p + 1) * w] = (
                jnp.dot(u, wyu_ref[p], preferred_element_type=F32)
                + jnp.dot(x, wyx_ref[p], preferred_element_type=F32))
            return carry
        lax.fori_loop(0, n_rc, emit, 0)


def s5_core(u_blk, ops, seq_rows):
    ws, wyu, wyx, dec = ops
    rows, width = u_blk.shape
    hp, w, s4 = ws.shape
    pb = S5_PAIR_BLOCK if hp % S5_PAIR_BLOCK == 0 else 1
    rc = _tile(rows, S5_ROW_CHUNK, 8)
    kern = functools.partial(_s5_kernel, seq_rows=tuple(seq_rows), pb=pb, rc=rc)
    wspec = lambda a, b: pl.BlockSpec((pb, a, b), lambda j: (j, 0, 0))
    return pl.pallas_call(
        kern,
        name="s5_core",
        grid=(hp // pb,),
        in_specs=[pl.BlockSpec((rows, pb * w), lambda j: (0, j)),
                  wspec(w, s4), wspec(w, w), wspec(s4, w), wspec(dec.shape[1], s4)],
        out_specs=pl.BlockSpec((rows, pb * w), lambda j: (0, j)),
        out_shape=jax.ShapeDtypeStruct((rows, width), F32),
        scratch_shapes=[pltpu.VMEM((rows, pb * w), F32)],
        compiler_params=_params(("parallel",)),
    )(u_blk, ws, wyu, wyx, dec)


def s5_layer(x, seqs, mix_g, in_w, ops, d_skip, glu_w, glu_b, out_w):
    e = glu_w.shape[0]
    p = ops[1].shape[-1] // (2 * S5_TB)
    h = rmsnorm(x, mix_g, BF16)
    uz = matmul(h, in_w.astype(BF16), out_dtype=F32)
    u_blk = s5_pack(uz, e, p)
    y_blk = s5_core(u_blk, ops, [(s // S5_TB, l // S5_TB) for s, l in seqs])
    y = s5_unpack(y_blk, uz, d_skip, e, p)
    gated = matmul(y, glu_w.astype(BF16), out_dtype=BF16, mode="glu",
                   extra=(glu_b.reshape(1, e).astype(F32), y, uz),
                   extra_specs=(_row_spec, _tile_spec(), _tile_spec(e)), tk=2048)
    return matmul(gated, out_w.astype(BF16), out_dtype=F32, mode="residual",
                  extra=(x,), extra_specs=(_tile_spec(),), tk=2048)


def _conv_qkv_kernel(flags_ref, xm_ref, prev_ref, next_ref, cw_ref, cb_ref, wq_ref, wk_ref, wkt_ref, wv_ref,
                     xc_ref, q_ref, k_ref, kt_ref, v_ref, pad_ref, *, ksize, k_scale):
    i = pl.program_id(0)
    tm = xm_ref.shape[0]
    half = ksize // 2
    xm = xm_ref[...]
    pad_ref[0:8, :] = prev_ref[...] * flags_ref[0, i].astype(F32)
    pad_ref[8:8 + tm, :] = xm
    pad_ref[8 + tm:16 + tm, :] = next_ref[...] * flags_ref[1, i].astype(F32)
    acc = cb_ref[...] + jnp.zeros_like(xm)
    for k in range(ksize):
        acc = acc + pad_ref[8 + k - half:8 + k - half + tm, :] * cw_ref[k:k + 1, :]
    xc = _silu(acc)
    xc_ref[...] = xc
    xcb = xc.astype(BF16)
    q_ref[...] = jnp.dot(xcb, wq_ref[0], preferred_element_type=F32).astype(q_ref.dtype)
    k_ref[...] = (jnp.dot(xcb, wk_ref[0], preferred_element_type=F32) * k_scale).astype(k_ref.dtype)
    kt = lax.dot_general(wkt_ref[0], xcb, (((1,), (1,)), ((), ())), preferred_element_type=F32)
    kt_ref[...] = (kt * k_scale).astype(kt_ref.dtype)
    v_ref[...] = jnp.dot(xm.astype(BF16), wv_ref[0], preferred_element_type=F32).astype(v_ref.dtype)


def _dense_block_diag(wb, width):
    nb, b, _ = wb.shape
    per = width // b
    tiles = wb.reshape(nb // per, per, b, b)
    eye = jnp.eye(per, dtype=wb.dtype)
    return jnp.einsum("tncd,nm->tncmd", tiles, eye).reshape(nb // per, width, width)


def conv_qkv(proj, seqs, conv_w, conv_b, wq, wk, wv, e, dk):
    t = proj.shape[0]
    ksize = conv_w.shape[0]
    cw = _tile(e, QKV_DENSE, LANES)
    lmin = functools.reduce(math.gcd, [l for _, l in seqs])
    tm = _tile(lmin, 1024, LANES)
    nt = t // tm
    starts = {s for s, _ in seqs}
    ends = {s + l for s, l in seqs}
    flags = jnp.array([[0 if i * tm in starts else 1 for i in range(nt)],
                       [0 if (i + 1) * tm in ends else 1 for i in range(nt)]], jnp.int32)
    wqd = _dense_block_diag(wq, cw).astype(BF16)
    wkd = _dense_block_diag(wk, cw).astype(BF16)
    wktd = jnp.swapaxes(wkd, 1, 2)
    wvd = _dense_block_diag(wv, cw).astype(BF16)
    r8 = tm // 8
    last8 = t // 8 - 1
    kern = functools.partial(_conv_qkv_kernel, ksize=ksize, k_scale=float(dk) ** -0.5)
    wspec = pl.BlockSpec((1, cw, cw), lambda i, c, f: (c, 0, 0))
    tile = pl.BlockSpec((tm, cw), lambda i, c, f: (i, c))
    outs = pl.pallas_call(
        kern,
        name="conv_qkv",
        grid_spec=pltpu.PrefetchScalarGridSpec(
            num_scalar_prefetch=1,
            grid=(nt, e // cw),
            in_specs=[tile,
                      pl.BlockSpec((8, cw), lambda i, c, f: (jnp.maximum(i * r8 - 1, 0), c)),
                      pl.BlockSpec((8, cw), lambda i, c, f: (jnp.minimum((i + 1) * r8, last8), c)),
                      pl.BlockSpec((ksize, cw), lambda i, c, f: (0, c)),
                      pl.BlockSpec((1, cw), lambda i, c, f: (0, c)),
                      wspec, wspec, wspec, wspec],
            out_specs=[tile, tile, tile, pl.BlockSpec((cw, tm), lambda i, c, f: (c, i)), tile],
            scratch_shapes=[pltpu.VMEM((tm + 16, cw), F32)]),
        out_shape=[jax.ShapeDtypeStruct((t, e), F32), jax.ShapeDtypeStruct((t, e), BF16),
                   jax.ShapeDtypeStruct((t, e), BF16), jax.ShapeDtypeStruct((e, t), BF16),
                   jax.ShapeDtypeStruct((t, e), BF16)],
        compiler_params=_params(("parallel", "parallel")),
    )(flags, proj, proj, proj, conv_w.astype(F32), conv_b.reshape(1, e).astype(F32), wqd, wkd, wktd, wvd)
    return outs


def _log_sigmoid(x):
    return jnp.minimum(x, 0.0) - jnp.log(1.0 + jnp.exp(-jnp.abs(x)))


def _mlstm_kernel(reset_ref, q_ref, k_ref, kt_ref, v_ref, ic_ref, fc_ref, ir_ref, fr_ref, h_ref,
                  c_ref, n_ref, m_ref, *, hb, dk):
    d = pl.program_id(0)
    c = pl.program_id(2)
    ch = q_ref.shape[0]
    hi = lax.Precision.HIGHEST

    @pl.when(reset_ref[d, c] == 1)
    def _():
        c_ref[...] = jnp.zeros_like(c_ref)
        n_ref[...] = jnp.zeros_like(n_ref)
        m_ref[...] = jnp.zeros_like(m_ref)

    tt = lax.broadcasted_iota(jnp.int32, (ch, ch), 0)
    ss = lax.broadcasted_iota(jnp.int32, (ch, ch), 1)
    sign = 1 - 2 * d
    allowed = (ss - tt) * sign <= 0
    tri = jnp.where(allowed, 1.0, 0.0).astype(F32)
    tri_t = jnp.where((tt - ss) * sign <= 0, 1.0, 0.0).astype(F32)

    lf_col = _log_sigmoid(fc_ref[0, 0])
    lf_row = _log_sigmoid(fr_ref[0, 0])
    bcum_col = jnp.dot(tri, lf_col, preferred_element_type=F32, precision=hi)
    bcum_row = jnp.dot(lf_row, tri_t, preferred_element_type=F32, precision=hi)
    i_col = ic_ref[0, 0]
    i_row = ir_ref[0, 0]
    blast_all = jnp.sum(lf_row, axis=-1, keepdims=True)

    for j in range(hb):
        lanes = slice(j * dk, (j + 1) * dk)
        bc = bcum_col[:, j:j + 1]
        ic = i_col[:, j:j + 1]
        br = bcum_row[j:j + 1, :]
        ir = i_row[j:j + 1, :]
        m_prev = m_ref[j, 0:1, 0:1]
        b_last = blast_all[j:j + 1, :]

        dmat = jnp.where(allowed, bc - br + ir, NEG_BIG)
        inter = bc + m_prev
        m_t = jnp.maximum(inter, jnp.max(dmat, axis=-1, keepdims=True))
        w_inter = jnp.exp(inter - m_t)
        qj = q_ref[:, lanes]
        s = jnp.dot(qj, kt_ref[lanes, :], preferred_element_type=F32) * jnp.exp(dmat - m_t)
        cmat = c_ref[j]
        nrow = n_ref[j]
        num = (w_inter * jnp.dot(qj, cmat.astype(BF16), preferred_element_type=F32)
               + jnp.dot(s.astype(BF16), v_ref[:, lanes], preferred_element_type=F32))
        den = (w_inter * jnp.sum(qj.astype(F32) * nrow, axis=-1, keepdims=True)
               + jnp.sum(s, axis=-1, keepdims=True))
        h_ref[0, :, lanes] = num * (1.0 / jnp.maximum(jnp.abs(den), jnp.exp(-m_t)))

        g_row = b_last - br + ir
        m_new = jnp.maximum(b_last + m_prev, jnp.max(g_row, axis=-1, keepdims=True))
        decay = jnp.exp(b_last + m_prev - m_new)
        wg_row = jnp.exp(g_row - m_new)
        wg_col = jnp.exp(b_last - bc + ic - m_new)
        kts = (kt_ref[lanes, :].astype(F32) * wg_row).astype(BF16)
        c_ref[j] = decay * cmat + jnp.dot(kts, v_ref[:, lanes], preferred_element_type=F32)
        n_ref[j] = decay * nrow + jnp.sum(k_ref[:, lanes].astype(F32) * wg_col, axis=0, keepdims=True)
        m_ref[j] = jnp.broadcast_to(m_new, m_ref.shape[1:])


def mlstm_core(q, k, kt, v, gates, seqs, nh):
    t, e = q.shape
    dk = e // nh
    hb = MLSTM_HEAD_BLOCK if nh % MLSTM_HEAD_BLOCK == 0 else nh
    nhb = nh // hb
    lmin = functools.reduce(math.gcd, [l for _, l in seqs])
    ch = _tile(lmin, MLSTM_CHUNK, LANES)
    nc = t // ch
    starts = {s for s, _ in seqs}
    ends = {s + l for s, l in seqs}
    reset = jnp.array([[1 if c * ch in starts else 0 for c in range(nc)],
                       [1 if (nc - c) * ch in ends else 0 for c in range(nc)]], jnp.int32)
    g4 = gates[:, :4 * nh].reshape(t, 4, nhb, hb)
    g_row = jnp.transpose(g4, (1, 2, 3, 0))
    g_col = jnp.pad(jnp.transpose(g4, (1, 2, 0, 3)), ((0, 0), (0, 0), (0, 0), (0, LANES - hb)))

    cidx = lambda d, c: c + d * (nc - 1 - 2 * c)
    tile = pl.BlockSpec((ch, hb * dk), lambda d, h, c, r: (cidx(d, c), h))
    colspec = lambda off: pl.BlockSpec((1, 1, ch, LANES), lambda d, h, c, r: (2 * d + off, h, cidx(d, c), 0))
    rowspec = lambda off: pl.BlockSpec((1, 1, hb, ch), lambda d, h, c, r: (2 * d + off, h, 0, cidx(d, c)))
    kern = functools.partial(_mlstm_kernel, hb=hb, dk=dk)
    return pl.pallas_call(
        kern,
        name="mlstm_core",
        grid_spec=pltpu.PrefetchScalarGridSpec(
            num_scalar_prefetch=1,
            grid=(2, nhb, nc),
            in_specs=[tile, tile,
                      pl.BlockSpec((hb * dk, ch), lambda d, h, c, r: (h, cidx(d, c))),
                      tile, colspec(0), colspec(1), rowspec(0), rowspec(1)],
            out_specs=pl.BlockSpec((1, ch, hb * dk), lambda d, h, c, r: (d, cidx(d, c), h)),
            scratch_shapes=[pltpu.VMEM((hb, dk, dk), F32), pltpu.VMEM((hb, 1, dk), F32),
                            pltpu.VMEM((hb, 8, LANES), F32)]),
        out_shape=jax.ShapeDtypeStruct((2, t, e), F32),
        compiler_params=_params(("parallel", "parallel", "arbitrary")),
    )(reset, q, k, kt, v, g_col, g_col, g_row, g_row)


def _mlstm_post_kernel(hf_ref, hb_ref, o_ref, z_ref, xc_ref, hn_ref, sk_ref, out_ref):
    hs = hf_ref[0] + hb_ref[0]
    hs = hs * lax.rsqrt(jnp.mean(hs * hs, axis=-1, keepdims=True) + EPS)
    hs = hs * hn_ref[...]
    hs = _sigmoid(o_ref[...]) * hs + sk_ref[...] * xc_ref[...]
    out_ref[...] = (hs * _silu(z_ref[...])).astype(out_ref.dtype)


def mlstm_post(hdir, proj, xc, head_norm, skip, e, nh):
    t = xc.shape[0]
    dk = e // nh
    tm = _tile(t, 1024, 8)
    hspec = lambda d: pl.BlockSpec((1, tm, dk), lambda i, h: (d, i, h))
    pspec = lambda off: pl.BlockSpec((tm, dk), lambda i, h: (i, h + off * nh))
    vec = pl.BlockSpec((1, dk), lambda i, h: (0, h))
    return pl.pallas_call(
        _mlstm_post_kernel,
        name="mlstm_post",
        grid=(t // tm, nh),
        in_specs=[hspec(0), hspec(1), pspec(2), pspec(1), pspec(0), vec, vec],
        out_specs=pl.BlockSpec((tm, dk), lambda i, h: (i, h)),
        out_shape=jax.ShapeDtypeStruct((t, e), BF16),
        compiler_params=_params(("parallel", "parallel")),
    )(hdir, hdir, proj, proj, xc, head_norm.reshape(1, e).astype(F32), skip.reshape(1, e).astype(F32))


def mlstm_layer(x, seqs, mix_g, in_w, gate_b, conv_w, conv_b, wq, wk, wv, skip, head_norm, out_w):
    e = out_w.shape[0]
    nh = gate_b.shape[0] // 4
    h = rmsnorm(x, mix_g, BF16)
    proj = matmul(h, in_w[:, :3 * e].astype(BF16), out_dtype=F32)
    gw = jnp.pad(in_w[:, 3 * e:], ((0, 0), (0, LANES - 4 * nh))).astype(BF16)
    gb = jnp.pad(gate_b, (0, LANES - 4 * nh)).reshape(1, LANES).astype(F32)
    gates = matmul(h, gw, out_dtype=F32, mode="bias", extra=(gb,), extra_specs=(_row_spec,))
    xc, q, k, kt, v = conv_qkv(proj, seqs, conv_w, conv_b, wq, wk, wv, e, e // nh)
    hdir = mlstm_core(q, k, kt, v, gates, seqs, nh)
    mixed = mlstm_post(hdir, proj, xc, head_norm, skip, e, nh)
    return matmul(mixed, out_w.astype(BF16), out_dtype=F32, mode="residual",
                  extra=(x,), extra_specs=(_tile_spec(),), tk=2048)


def _xattn_kernel(seq_ref, q_ref, k_ref, v_ref, o_ref, *, scale):
    del seq_ref
    s = lax.dot_general(q_ref[...], k_ref[0], (((1,), (1,)), ((), ())), preferred_element_type=F32) * scale
    s = s - jnp.max(s, axis=-1, keepdims=True)
    p = jnp.exp(s)
    p = p / jnp.sum(p, axis=-1, keepdims=True)
    o_ref[...] = jnp.dot(p.astype(BF16), v_ref[0], preferred_element_type=F32).astype(o_ref.dtype)


def xattn_core(q, kmem, vmem, seqs):
    t, d = q.shape
    hd = d // XA_HEADS
    lmin = functools.reduce(math.gcd, [l for _, l in seqs])
    tm = _tile(lmin, 512, 8)
    seq_of = jnp.array([max(j for j, (s, _) in enumerate(seqs) if s <= i * tm) for i in range(t // tm)],
                       jnp.int32)
    m = kmem.shape[1]
    kern = functools.partial(_xattn_kernel, scale=float(hd) ** -0.5)
    mem_spec = pl.BlockSpec((1, m, hd), lambda i, h, sq: (sq[i], 0, h))
    return pl.pallas_call(
        kern,
        name="xattn_core",
        grid_spec=pltpu.PrefetchScalarGridSpec(
            num_scalar_prefetch=1,
            grid=(t // tm, XA_HEADS),
            in_specs=[pl.BlockSpec((tm, hd), lambda i, h, sq: (i, h)), mem_spec, mem_spec],
            out_specs=pl.BlockSpec((tm, hd), lambda i, h, sq: (i, h))),
        out_shape=jax.ShapeDtypeStruct((t, d), BF16),
        compiler_params=_params(("parallel", "parallel")),
    )(seq_of, q, kmem, vmem)


def xattn_layer(x, mem, seqs, xa_g, mem_g, wq, wk, wv, wo):
    ns, m, d = mem.shape
    hq = rmsnorm(x, xa_g, BF16)
    mem_n = rmsnorm(mem.reshape(ns * m, d), mem_g, BF16)
    q = matmul(hq, wq.astype(BF16), out_dtype=BF16)
    kmem = matmul(mem_n, wk.astype(BF16), out_dtype=BF16).reshape(ns, m, d)
    vmem = matmul(mem_n, wv.astype(BF16), out_dtype=BF16).reshape(ns, m, d)
    o = xattn_core(q, kmem, vmem, seqs)
    return matmul(o, wo.astype(BF16), out_dtype=F32, mode="residual", extra=(x,), extra_specs=(_tile_spec(),))


def kernel(x_prompt, x_sample, mem_prompt, mem_sample, mix_norm, a_in_w, s5_a_re, s5_a_im, s5_log_step,
           s5_b_re, s5_b_im, s5_c_re, s5_c_im, s5_d, a_glu_w, a_glu_b, a_out_w, b_in_w, b_gate_bias,
           b_conv_w, b_conv_b, b_wq, b_wk, b_wv, b_skip, b_head_norm, b_out_w, xa_norm, mem_norm,
           xa_wq, xa_wk, xa_wv, xa_wo, final_norm):
    bp, lp, d = x_prompt.shape
    bs, ls, _ = x_sample.shape
    depth = mix_norm.shape[0]
    seqs = [(i * lp, lp) for i in range(bp)] + [(bp * lp + i * ls, ls) for i in range(bs)]
    x = jnp.concatenate([x_prompt.reshape(bp * lp, d), x_sample.reshape(bs * ls, d)], axis=0)
    mem = jnp.concatenate([mem_prompt, mem_sample], axis=0)
    s5_ops = jax.vmap(_s5_operators)(s5_a_re, s5_a_im, s5_log_step, s5_b_re, s5_b_im, s5_c_re, s5_c_im)
    for i in range(depth):
        j = i // 2
        if i % 2 == 0:
            ops = tuple(o[j] for o in s5_ops)
            x = s5_layer(x, seqs, mix_norm[i], a_in_w[j], ops, s5_d[j], a_glu_w[j], a_glu_b[j], a_out_w[j])
        else:
            x = mlstm_layer(x, seqs, mix_norm[i], b_in_w[j], b_gate_bias[j], b_conv_w[j], b_conv_b[j],
                            b_wq[j], b_wk[j], b_wv[j], b_skip[j], b_head_norm[j], b_out_w[j])
        x = xattn_layer(x, mem, seqs, xa_norm[i], mem_norm[i], xa_wq[i], xa_wk[i], xa_wv[i], xa_wo[i])
    y = rmsnorm(x, final_norm, F32)
    return (y[:bp * lp].reshape(bp, lp, d), y[bp * lp:].reshape(bs, ls, d))
```

```python
import functools
import math

import jax
import jax.numpy as jnp
from jax import lax
from jax.experimental import pallas as pl
from jax.experimental.pallas import tpu as pltpu

F32 = jnp.float32
BF16 = jnp.bfloat16

XA_HEADS = 4
EPS = 1e-6
S5_TB = 16
S5_PAIR_BLOCK = 2
S5_ROW_CHUNK = 256
S5_RELAYOUT_ROWS = 128
S5_RELAYOUT_CHUNK = 32
MLSTM_CHUNK = 256
MLSTM_HEAD_BLOCK = 8
QKV_DENSE = 256
LANES = 128
MXU_COLS = 256
SUBLANES = 8
V7X_VMEM_LIMIT = 56 * 1024 * 1024
NEG_BIG = -1e30


def _tile(n, pref, mult):
    best = None
    for t in range(mult, min(n, pref) + 1, mult):
        if n % t == 0:
            best = t
    return n if best is None else best


def _params(sem):
    return pltpu.CompilerParams(dimension_semantics=sem, vmem_limit_bytes=V7X_VMEM_LIMIT)


def _rmsnorm_kernel(x_ref, g_ref, o_ref):
    x = x_ref[...].astype(F32)
    ms = jnp.mean(x * x, axis=-1, keepdims=True)
    o_ref[...] = (x * lax.rsqrt(ms + EPS) * g_ref[...]).astype(o_ref.dtype)


def rmsnorm(x, g, out_dtype):
    t, d = x.shape
    tm = _tile(t, 512, 8)
    return pl.pallas_call(
        _rmsnorm_kernel,
        name="rmsnorm",
        grid=(t // tm,),
        in_specs=[pl.BlockSpec((tm, d), lambda i: (i, 0)),
                  pl.BlockSpec((1, d), lambda i: (0, 0))],
        out_specs=pl.BlockSpec((tm, d), lambda i: (i, 0)),
        out_shape=jax.ShapeDtypeStruct((t, d), out_dtype),
        compiler_params=_params(("parallel",)),
    )(x, g.reshape(1, d).astype(F32))


def _sigmoid(x):
    return 1.0 / (1.0 + jnp.exp(-x))


def _silu(x):
    return x * _sigmoid(x)


def _mm_epilogue(mode, acc, extra, cols):
    if mode == "plain":
        return acc
    if mode == "bias":
        return acc + extra[0][:, cols]
    if mode == "residual":
        return extra[0][:, cols] + acc
    if mode == "glu":
        b_ref, y_ref, z_ref = extra
        y = y_ref[:, cols].astype(F32)
        return y * _sigmoid(acc + b_ref[:, cols]) * _silu(z_ref[:, cols])
    raise ValueError(mode)


def _mm_kernel(*refs, nk, mode, n_extra):
    a_ref, w_ref = refs[0], refs[1]
    extra = refs[2:2 + n_extra]
    o_ref = refs[2 + n_extra]
    acc_ref = refs[3 + n_extra] if nk > 1 else None
    tn = o_ref.shape[1]
    cw = MXU_COLS if tn % MXU_COLS == 0 else tn

    def finish():
        for c0 in range(0, tn, cw):
            cols = slice(c0, c0 + cw)
            acc = jnp.dot(a_ref[...], w_ref[:, cols], preferred_element_type=F32)
            if acc_ref is not None:
                acc = acc_ref[:, cols] + acc
            o_ref[:, cols] = _mm_epilogue(mode, acc, extra, cols).astype(o_ref.dtype)

    if nk == 1:
        finish()
        return
    k = pl.program_id(2)

    @pl.when(k == 0)
    def _():
        acc_ref[...] = jnp.dot(a_ref[...], w_ref[...], preferred_element_type=F32)

    @pl.when(jnp.logical_and(k > 0, k < nk - 1))
    def _():
        acc_ref[...] += jnp.dot(a_ref[...], w_ref[...], preferred_element_type=F32)

    pl.when(k == nk - 1)(finish)


def matmul(a, w, *, out_dtype, mode="plain", extra=(), extra_specs=(), tm=1024, tn=1024, tk=4096):
    m, kd = a.shape
    n = w.shape[1]
    tm, tn, tk = _tile(m, tm, 8), _tile(n, tn, LANES), _tile(kd, tk, LANES)
    nk = kd // tk
    kern = functools.partial(_mm_kernel, nk=nk, mode=mode, n_extra=len(extra))
    in_specs = [pl.BlockSpec((tm, tk), lambda i, j, k: (i, k)),
                pl.BlockSpec((tk, tn), lambda i, j, k: (k, j))]
    in_specs += [mk(tm, tn) for mk in extra_specs]
    return pl.pallas_call(
        kern,
        name="matmul_" + mode,
        grid=(m // tm, n // tn, nk),
        in_specs=in_specs,
        out_specs=pl.BlockSpec((tm, tn), lambda i, j, k: (i, j)),
        out_shape=jax.ShapeDtypeStruct((m, n), out_dtype),
        scratch_shapes=[pltpu.VMEM((tm, tn), F32)] if nk > 1 else [],
        compiler_params=_params(("parallel", "parallel", "arbitrary")),
    )(a, w, *extra)


def _row_spec(tm, tn):
    return pl.BlockSpec((1, tn), lambda i, j, k: (0, j))


def _tile_spec(col_off=0):
    def mk(tm, tn):
        off = col_off // tn
        return pl.BlockSpec((tm, tn), lambda i, j, k: (i, j + off))
    return mk


def _s5_operators(a_re, a_im, log_step, b_re, b_im, c_re, c_im):
    hi = lax.Precision.HIGHEST
    tb = S5_TB
    _, g, n = a_re.shape
    p = b_re.shape[-1]
    hp, m = g // 2, tb * p
    lam_re = jnp.minimum(a_re.astype(F32), -1e-4)
    lam_im = a_im.astype(F32)
    dt = jnp.exp(log_step.astype(F32))[..., None]
    mag = jnp.exp(lam_re * dt)
    ab_re = mag * jnp.cos(lam_im * dt)
    ab_im = mag * jnp.sin(lam_im * dt)
    den = lam_re * lam_re + lam_im * lam_im
    nr, ni = ab_re - 1.0, ab_im
    f_re = (nr * lam_re + ni * lam_im) / den
    f_im = (ni * lam_re - nr * lam_im) / den
    b_re, b_im = b_re.astype(F32), b_im.astype(F32)
    bb_re = f_re[..., None] * b_re - f_im[..., None] * b_im
    bb_im = f_re[..., None] * b_im + f_im[..., None] * b_re
    c_re, c_im = c_re.astype(F32), c_im.astype(F32)

    def powers(steps):
        pmag = jnp.exp((lam_re * dt)[..., None] * steps)
        pang = (lam_im * dt)[..., None] * steps
        return pmag * jnp.cos(pang), pmag * jnp.sin(pang)

    pw_re, pw_im = powers(jnp.arange(tb + 1, dtype=F32))
    rw_re, rw_im = powers(jnp.arange(SUBLANES + 1, dtype=F32) * tb)

    def c_times_powers(cr, ci, pr, pi):
        ct_re = jnp.tile(jnp.swapaxes(cr, 1, 2), (1, 1, tb + 1))
        ct_im = jnp.tile(jnp.swapaxes(ci, 1, 2), (1, 1, tb + 1))
        rp_re, rp_im = jnp.repeat(pr, p, axis=-1), jnp.repeat(pi, p, axis=-1)
        return ct_re * rp_re - ct_im * rp_im, ct_re * rp_im + ct_im * rp_re

    ef_re, ef_im = c_times_powers(c_re[0], c_im[0], pw_re[0], pw_im[0])
    eb_re, eb_im = c_times_powers(c_re[1], c_im[1], pw_re[1][..., ::-1], pw_im[1][..., ::-1])

    def lag_rows(br, bi, er, ei):
        return (jnp.einsum("gnq,gnm->gqm", br, er, precision=hi)
                - jnp.einsum("gnq,gnm->gqm", bi, ei, precision=hi))

    kt_f = lag_rows(bb_re[0], bb_im[0], ef_re[..., :m], ef_im[..., :m])
    kt_b = lag_rows(bb_re[1], bb_im[1], eb_re[..., p:], eb_im[..., p:])
    zpad = jnp.zeros_like(kt_f)
    kf_pad = jnp.concatenate([zpad, kt_f], axis=-1)
    kb_pad = jnp.concatenate([kt_b, zpad], axis=-1)
    kloc = jnp.stack([kf_pad[..., m - p * s:2 * m - p * s] + kb_pad[..., (tb - 1 - s) * p:(tb - 1 - s) * p + m]
                      for s in range(tb)], axis=1).reshape(g, m, m)

    cf_re, cf_im = ef_re[..., p:], -ef_im[..., p:]
    cb_re, cb_im = eb_re[..., :m], -eb_im[..., :m]

    def state_in(pr, pi, br, bi):
        pr_t, pi_t = jnp.swapaxes(pr, 1, 2)[:, :, None, :], jnp.swapaxes(pi, 1, 2)[:, :, None, :]
        br_t, bi_t = jnp.swapaxes(br, 1, 2)[:, None, :, :], jnp.swapaxes(bi, 1, 2)[:, None, :, :]
        return ((pr_t * br_t - pi_t * bi_t).reshape(g, m, n), (pr_t * bi_t + pi_t * br_t).reshape(g, m, n))

    sf_re, sf_im = state_in(pw_re[0][..., tb - 1::-1], pw_im[0][..., tb - 1::-1], bb_re[0], bb_im[0])
    sb_re, sb_im = state_in(pw_re[1][..., :tb], pw_im[1][..., :tb], bb_re[1], bb_im[1])

    def block_diag2(a):
        a = a.reshape((hp, 2) + a.shape[1:])
        z = jnp.zeros_like(a[:, 0])
        return jnp.concatenate([jnp.concatenate([a[:, 0], z], axis=-1),
                                jnp.concatenate([z, a[:, 1]], axis=-1)], axis=1)

    wyu = block_diag2(kloc)
    ws = jnp.concatenate([block_diag2(c) for c in (sf_re, sf_im, sb_re, sb_im)], axis=-1)
    wyx = jnp.concatenate([block_diag2(c) for c in (cf_re, cf_im, cb_re, cb_im)], axis=1)

    def pair_rows(a):
        return jnp.transpose(a.reshape(hp, 2, n, a.shape[-1]), (0, 3, 1, 2)).reshape(hp, a.shape[-1], 2 * n)

    asc = jnp.concatenate([pair_rows(c) for c in (rw_re[0], rw_im[0], rw_re[1], rw_im[1])], axis=-1)
    dec = jnp.concatenate([asc, jnp.zeros((hp, 2 * SUBLANES - asc.shape[1], 8 * n), F32),
                           asc[:, SUBLANES - 1::-1]], axis=1)
    return ws.astype(BF16), wyu.astype(BF16), wyx.astype(BF16), dec


def _piece_transpose(vs, slot, p):
    vs = list(vs)
    d = len(vs) // 2
    while d >= 1:
        upper = (slot & d) != 0
        for i in range(len(vs)):
            if i & d == 0:
                a, b = vs[i], vs[i + d]
                vs[i] = jnp.where(upper, pltpu.roll(b, d * p, axis=1), a)
                vs[i + d] = jnp.where(upper, b, pltpu.roll(a, LANES - d * p, axis=1))
        d //= 2
    return vs


def _s5_pack_kernel(u_ref, o_ref, *, tb, p):
    gpv = LANES // p
    halves = tb // gpv
    rs = S5_RELAYOUT_CHUNK
    slot = lax.broadcasted_iota(jnp.int32, (rs, LANES), 1) // p
    for r0 in range(0, o_ref.shape[0], rs):
        for c in range(halves):
            xs = [u_ref[pl.ds(r0 * tb + c * gpv + j, rs, stride=tb), :] for j in range(gpv)]
            for g in range(gpv):
                acc = None
                for j in range(gpv):
                    r = xs[j] if j == g else pltpu.roll(xs[j], (p * (j - g)) % LANES, axis=1)
                    acc = r if acc is None else jnp.where(slot == j, r, acc)
                lane0 = (g * halves + c) * LANES
                o_ref[r0:r0 + rs, lane0:lane0 + LANES] = acc.astype(o_ref.dtype)


def _s5_unpack_kernel(y_ref, u_ref, d_ref, o_ref, nat_ref, *, tb, p):
    gpv = LANES // p
    halves = tb // gpv
    rs = S5_RELAYOUT_CHUNK
    slot = lax.broadcasted_iota(jnp.int32, (rs, LANES), 1) // p
    for r0 in range(0, y_ref.shape[0], rs):
        for c in range(halves):
            chunks = [y_ref[r0:r0 + rs, (g * halves + c) * LANES:(g * halves + c + 1) * LANES] for g in range(gpv)]
            for j, v in enumerate(_piece_transpose(chunks, slot, p)):
                rows = pl.ds(r0 * tb + c * gpv + j, rs, stride=tb)
                nat_ref[rows, :] = jax.nn.gelu(v + d_ref[...] * u_ref[rows, :])
    o_ref[...] = nat_ref[...].astype(o_ref.dtype)


def _relayout_blocks(t, e, tb):
    rb = _tile(t // tb, S5_RELAYOUT_ROWS, S5_RELAYOUT_CHUNK)
    assert rb % S5_RELAYOUT_CHUNK == 0 and e % LANES == 0
    return rb, LANES


def s5_pack(uz, e, p):
    t, tb = uz.shape[0], S5_TB
    rb, w = _relayout_blocks(t, e, tb)
    return pl.pallas_call(
        functools.partial(_s5_pack_kernel, tb=tb, p=p),
        name="s5_pack",
        grid=(t // tb // rb, e // w),
        in_specs=[pl.BlockSpec((rb * tb, w), lambda i, j: (i, j))],
        out_specs=pl.BlockSpec((rb, w * tb), lambda i, j: (i, j)),
        out_shape=jax.ShapeDtypeStruct((t // tb, e * tb), BF16),
        compiler_params=_params(("parallel", "parallel")),
    )(uz)


def s5_unpack(y_blk, uz, d_skip, e, p):
    t, tb = uz.shape[0], S5_TB
    rb, w = _relayout_blocks(t, e, tb)
    return pl.pallas_call(
        functools.partial(_s5_unpack_kernel, tb=tb, p=p),
        name="s5_unpack",
        grid=(t // tb // rb, e // w),
        in_specs=[pl.BlockSpec((rb, w * tb), lambda i, j: (i, j)),
                  pl.BlockSpec((rb * tb, w), lambda i, j: (i, j)),
                  pl.BlockSpec((1, w), lambda i, j: (0, j))],
        out_specs=pl.BlockSpec((rb * tb, w), lambda i, j: (i, j)),
        out_shape=jax.ShapeDtypeStruct((t, e), BF16),
        scratch_shapes=[pltpu.VMEM((rb * tb, w), F32)],
        compiler_params=_params(("parallel", "parallel")),
    )(y_blk, uz, d_skip.reshape(1, e).astype(F32))


def _cmul(ar, ai, br, bi):
    return ar * br - ai * bi, ar * bi + ai * br


def _tile_scan(sr, si, xr, xi, d1, d2, d4, d8, pk, row, reverse):
    def shifted(v, dist):
        if reverse:
            return jnp.where(row < SUBLANES - dist, pltpu.roll(v, SUBLANES - dist, axis=0), 0.0)
        return jnp.where(row >= dist, pltpu.roll(v, dist, axis=0), 0.0)

    ir, ii = sr, si
    for dist, (dr, di) in ((1, d1), (2, d2), (4, d4)):
        mr, mi = _cmul(dr, di, shifted(ir, dist), shifted(ii, dist))
        ir, ii = ir + mr, ii + mi
    cr, ci = _cmul(pk[0], pk[1], xr, xi)
    orr, oi = _cmul(d8[0], d8[1], xr, xi)
    last = 0 if reverse else SUBLANES - 1
    return (shifted(ir, 1) + cr, shifted(ii, 1) + ci,
            ir[last:last + 1] + orr, ii[last:last + 1] + oi)


def _s5_kernel(u_ref, ws_ref, wyu_ref, wyx_ref, dec_ref, y_ref, s_ref, *, seq_rows, pb, rc):
    rows = u_ref.shape[0]
    w = ws_ref.shape[1]
    sn = ws_ref.shape[2] // 4
    n_rc = rows // rc

    for p in range(pb):
        def fill(c, carry, p=p):
            r0 = pl.multiple_of(c * rc, rc)
            u = u_ref[pl.ds(r0, rc), p * w:(p + 1) * w]
            s_ref[pl.ds(r0, rc), p * w:(p + 1) * w] = jnp.dot(u, ws_ref[p], preferred_element_type=F32)
            return carry
        lax.fori_loop(0, n_rc, fill, 0)

    row = lax.broadcasted_iota(jnp.int32, (SUBLANES, sn), 0)

    def tables(p, c_re, rows0):
        re, im = slice(c_re * sn, (c_re + 1) * sn), slice((c_re + 1) * sn, (c_re + 2) * sn)
        at = lambda k: (dec_ref[p, k:k + 1, re], dec_ref[p, k:k + 1, im])
        return at(1), at(2), at(4), at(8), (dec_ref[p, rows0:rows0 + SUBLANES, re],
                                             dec_ref[p, rows0:rows0 + SUBLANES, im])

    by_len = {}
    for start, length in seq_rows:
        by_len.setdefault(length, []).append(start)
    for length, starts in by_len.items():
        chains = [(st, p) for st in starts for p in range(pb)]
        tabs = [(tables(p, 0, 0), tables(p, 2, 2 * SUBLANES)) for p in range(pb)]

        def step(k8, carry, chains=chains, length=length, tabs=tabs):
            out = []
            for (st, p), (fr, fi, br, bi) in zip(chains, carry):
                rf = pl.multiple_of(st + k8 * SUBLANES, SUBLANES)
                rb = pl.multiple_of(st + length - SUBLANES - k8 * SUBLANES, SUBLANES)
                cols = [slice(p * w + c * sn, p * w + (c + 1) * sn) for c in range(4)]
                xfr, xfi, fr, fi = _tile_scan(s_ref[pl.ds(rf, SUBLANES), cols[0]], s_ref[pl.ds(rf, SUBLANES), cols[1]],
                                              fr, fi, *tabs[p][0], row, False)
                xbr, xbi, br, bi = _tile_scan(s_ref[pl.ds(rb, SUBLANES), cols[2]], s_ref[pl.ds(rb, SUBLANES), cols[3]],
                                              br, bi, *tabs[p][1], row, True)
                s_ref[pl.ds(rf, SUBLANES), cols[0]] = xfr
                s_ref[pl.ds(rf, SUBLANES), cols[1]] = xfi
                s_ref[pl.ds(rb, SUBLANES), cols[2]] = xbr
                s_ref[pl.ds(rb, SUBLANES), cols[3]] = xbi
                out.append((fr, fi, br, bi))
            return tuple(out)

        assert length % SUBLANES == 0 and all(st % SUBLANES == 0 for st in starts)
        zero = jnp.zeros((1, sn), F32)
        lax.fori_loop(0, length // SUBLANES, step, tuple((zero, zero, zero, zero) for _ in chains))

    for p in range(pb):
        def emit(c, carry, p=p):
            r0 = pl.multiple_of(c * rc, rc)
            u = u_ref[pl.ds(r0, rc), p * w:(p + 1) * w]
            x = s_ref[pl.ds(r0, rc), p * w:(p + 1) * w].astype(BF16)
            y_ref[pl.ds(r0, rc), p * w:(p + 1) * w] = (
                jnp.dot(u, wyu_ref[p], preferred_element_type=F32)
                + jnp.dot(x, wyx_ref[p], preferred_element_type=F32))
            return carry
        lax.fori_loop(0, n_rc, emit, 0)


def s5_core(u_blk, ops, layer, seq_rows):
    ws, wyu, wyx, dec = ops
    rows, width = u_blk.shape
    _, hp, w, s4 = ws.shape
    pb = S5_PAIR_BLOCK if hp % S5_PAIR_BLOCK == 0 else 1
    rc = _tile(rows, S5_ROW_CHUNK, 8)
    kern = functools.partial(_s5_kernel, seq_rows=tuple(seq_rows), pb=pb, rc=rc)
    wspec = lambda a, b: pl.BlockSpec((None, pb, a, b), lambda j: (layer, j, 0, 0))
    return pl.pallas_call(
        kern,
        name="s5_core",
        grid=(hp // pb,),
        in_specs=[pl.BlockSpec((rows, pb * w), lambda j: (0, j)),
                  wspec(w, s4), wspec(w, w), wspec(s4, w), wspec(dec.shape[2], s4)],
        out_specs=pl.BlockSpec((rows, pb * w), lambda j: (0, j)),
        out_shape=jax.ShapeDtypeStruct((rows, width), F32),
        scratch_shapes=[pltpu.VMEM((rows, pb * w), F32)],
        compiler_params=_params(("parallel",)),
    )(u_blk, ws, wyu, wyx, dec)


def s5_layer(x, seqs, mix_g, in_w, ops, layer, d_skip, glu_w, glu_b, out_w):
    e = glu_w.shape[0]
    p = ops[1].shape[-1] // (2 * S5_TB)
    h = rmsnorm(x, mix_g, BF16)
    uz = matmul(h, in_w.astype(BF16), out_dtype=F32)
    u_blk = s5_pack(uz, e, p)
    y_blk = s5_core(u_blk, ops, layer, [(s // S5_TB, l // S5_TB) for s, l in seqs])
    y = s5_unpack(y_blk, uz, d_skip, e, p)
    gated = matmul(y, glu_w.astype(BF16), out_dtype=BF16, mode="glu",
                   extra=(glu_b.reshape(1, e).astype(F32), y, uz),
                   extra_specs=(_row_spec, _tile_spec(), _tile_spec(e)), tk=2048)
    return matmul(gated, out_w.astype(BF16), out_dtype=F32, mode="residual",
                  extra=(x,), extra_specs=(_tile_spec(),), tk=2048)


def _conv_qkv_kernel(flags_ref, xm_ref, prev_ref, next_ref, cw_ref, cb_ref, wq_ref, wk_ref, wkt_ref, wv_ref,
                     xc_ref, q_ref, k_ref, kt_ref, v_ref, pad_ref, *, ksize, k_scale):
    i = pl.program_id(0)
    tm = xm_ref.shape[0]
    half = ksize // 2
    xm = xm_ref[...]
    pad_ref[0:8, :] = prev_ref[...] * flags_ref[0, i].astype(F32)
    pad_ref[8:8 + tm, :] = xm
    pad_ref[8 + tm:16 + tm, :] = next_ref[...] * flags_ref[1, i].astype(F32)
    acc = cb_ref[...] + jnp.zeros_like(xm)
    for k in range(ksize):
        acc = acc + pad_ref[8 + k - half:8 + k - half + tm, :] * cw_ref[k:k + 1, :]
    xc = _silu(acc)
    xc_ref[...] = xc
    xcb = xc.astype(BF16)
    q_ref[...] = jnp.dot(xcb, wq_ref[0], preferred_element_type=F32).astype(q_ref.dtype)
    k_ref[...] = (jnp.dot(xcb, wk_ref[0], preferred_element_type=F32) * k_scale).astype(k_ref.dtype)
    kt = lax.dot_general(wkt_ref[0], xcb, (((1,), (1,)), ((), ())), preferred_element_type=F32)
    kt_ref[...] = (kt * k_scale).astype(kt_ref.dtype)
    v_ref[...] = jnp.dot(xm.astype(BF16), wv_ref[0], preferred_element_type=F32).astype(v_ref.dtype)


def _dense_block_diag(wb, width):
    nb, b, _ = wb.shape
    per = width // b
    tiles = wb.reshape(nb // per, per, b, b)
    eye = jnp.eye(per, dtype=wb.dtype)
    return jnp.einsum("tncd,nm->tncmd", tiles, eye).reshape(nb // per, width, width)


def conv_qkv(proj, seqs, conv_w, conv_b, wq, wk, wv, e, dk):
    t = proj.shape[0]
    ksize = conv_w.shape[0]
    cw = _tile(e, QKV_DENSE, LANES)
    lmin = functools.reduce(math.gcd, [l for _, l in seqs])
    tm = _tile(lmin, 1024, LANES)
    nt = t // tm
    starts = {s for s, _ in seqs}
    ends = {s + l for s, l in seqs}
    flags = jnp.array([[0 if i * tm in starts else 1 for i in range(nt)],
                       [0 if (i + 1) * tm in ends else 1 for i in range(nt)]], jnp.int32)
    wqd = _dense_block_diag(wq, cw).astype(BF16)
    wkd = _dense_block_diag(wk, cw).astype(BF16)
    wktd = jnp.swapaxes(wkd, 1, 2)
    wvd = _dense_block_diag(wv, cw).astype(BF16)
    r8 = tm // 8
    last8 = t // 8 - 1
    kern = functools.partial(_conv_qkv_kernel, ksize=ksize, k_scale=float(dk) ** -0.5)
    wspec = pl.BlockSpec((1, cw, cw), lambda i, c, f: (c, 0, 0))
    tile = pl.BlockSpec((tm, cw), lambda i, c, f: (i, c))
    outs = pl.pallas_call(
        kern,
        name="conv_qkv",
        grid_spec=pltpu.PrefetchScalarGridSpec(
            num_scalar_prefetch=1,
            grid=(nt, e // cw),
            in_specs=[tile,
                      pl.BlockSpec((8, cw), lambda i, c, f: (jnp.maximum(i * r8 - 1, 0), c)),
                      pl.BlockSpec((8, cw), lambda i, c, f: (jnp.minimum((i + 1) * r8, last8), c)),
                      pl.BlockSpec((ksize, cw), lambda i, c, f: (0, c)),
                      pl.BlockSpec((1, cw), lambda i, c, f: (0, c)),
                      wspec, wspec, wspec, wspec],
            out_specs=[tile, tile, tile, pl.BlockSpec((cw, tm), lambda i, c, f: (c, i)), tile],
            scratch_shapes=[pltpu.VMEM((tm + 16, cw), F32)]),
        out_shape=[jax.ShapeDtypeStruct((t, e), F32), jax.ShapeDtypeStruct((t, e), BF16),
                   jax.ShapeDtypeStruct((t, e), BF16), jax.ShapeDtypeStruct((e, t), BF16),
                   jax.ShapeDtypeStruct((t, e), BF16)],
        compiler_params=_params(("parallel", "parallel")),
    )(flags, proj, proj, proj, conv_w.astype(F32), conv_b.reshape(1, e).astype(F32), wqd, wkd, wktd, wvd)
    return outs


def _log_sigmoid(x):
    return jnp.minimum(x, 0.0) - jnp.log(1.0 + jnp.exp(-jnp.abs(x)))


def _mlstm_kernel(reset_ref, q_ref, k_ref, kt_ref, v_ref, ic_ref, fc_ref, ir_ref, fr_ref, h_ref,
                  c_ref, n_ref, m_ref, *, hb, dk):
    d = pl.program_id(0)
    c = pl.program_id(2)
    ch = q_ref.shape[0]
    hi = lax.Precision.HIGHEST

    @pl.when(reset_ref[d, c] == 1)
    def _():
        c_ref[...] = jnp.zeros_like(c_ref)
        n_ref[...] = jnp.zeros_like(n_ref)
        m_ref[...] = jnp.zeros_like(m_ref)

    tt = lax.broadcasted_iota(jnp.int32, (ch, ch), 0)
    ss = lax.broadcasted_iota(jnp.int32, (ch, ch), 1)
    sign = 1 - 2 * d
    allowed = (ss - tt) * sign <= 0
    tri = jnp.where(allowed, 1.0, 0.0).astype(F32)
    tri_t = jnp.where((tt - ss) * sign <= 0, 1.0, 0.0).astype(F32)

    lf_col = _log_sigmoid(fc_ref[0, 0])
    lf_row = _log_sigmoid(fr_ref[0, 0])
    bcum_col = jnp.dot(tri, lf_col, preferred_element_type=F32, precision=hi)
    bcum_row = jnp.dot(lf_row, tri_t, preferred_element_type=F32, precision=hi)
    i_col = ic_ref[0, 0]
    i_row = ir_ref[0, 0]
    blast_all = jnp.sum(lf_row, axis=-1, keepdims=True)

    for j in range(hb):
        lanes = slice(j * dk, (j + 1) * dk)
        bc = bcum_col[:, j:j + 1]
        ic = i_col[:, j:j + 1]
        br = bcum_row[j:j + 1, :]
        ir = i_row[j:j + 1, :]
        m_prev = m_ref[j, 0:1, 0:1]
        b_last = blast_all[j:j + 1, :]

        dmat = jnp.where(allowed, bc - br + ir, NEG_BIG)
        inter = bc + m_prev
        m_t = jnp.maximum(inter, jnp.max(dmat, axis=-1, keepdims=True))
        w_inter = jnp.exp(inter - m_t)
        qj = q_ref[:, lanes]
        s = jnp.dot(qj, kt_ref[lanes, :], preferred_element_type=F32) * jnp.exp(dmat - m_t)
        cmat = c_ref[j]
        nrow = n_ref[j]
        num = (w_inter * jnp.dot(qj, cmat.astype(BF16), preferred_element_type=F32)
               + jnp.dot(s.astype(BF16), v_ref[:, lanes], preferred_element_type=F32))
        den = (w_inter * jnp.sum(qj.astype(F32) * nrow, axis=-1, keepdims=True)
               + jnp.sum(s, axis=-1, keepdims=True))
        h_ref[0, :, lanes] = num * (1.0 / jnp.maximum(jnp.abs(den), jnp.exp(-m_t)))

        g_row = b_last - br + ir
        m_new = jnp.maximum(b_last + m_prev, jnp.max(g_row, axis=-1, keepdims=True))
        decay = jnp.exp(b_last + m_prev - m_new)
        wg_row = jnp.exp(g_row - m_new)
        wg_col = jnp.exp(b_last - bc + ic - m_new)
        kts = (kt_ref[lanes, :].astype(F32) * wg_row).astype(BF16)
        c_ref[j] = decay * cmat + jnp.dot(kts, v_ref[:, lanes], preferred_element_type=F32)
        n_ref[j] = decay * nrow + jnp.sum(k_ref[:, lanes].astype(F32) * wg_col, axis=0, keepdims=True)
        m_ref[j] = jnp.broadcast_to(m_new, m_ref.shape[1:])


def mlstm_core(q, k, kt, v, gates, seqs, nh):
    t, e = q.shape
    dk = e // nh
    hb = MLSTM_HEAD_BLOCK if nh % MLSTM_HEAD_BLOCK == 0 else nh
    nhb = nh // hb
    lmin = functools.reduce(math.gcd, [l for _, l in seqs])
    ch = _tile(lmin, MLSTM_CHUNK, LANES)
    nc = t // ch
    starts = {s for s, _ in seqs}
    ends = {s + l for s, l in seqs}
    reset = jnp.array([[1 if c * ch in starts else 0 for c in range(nc)],
                       [1 if (nc - c) * ch in ends else 0 for c in range(nc)]], jnp.int32)
    g4 = gates[:, :4 * nh].reshape(t, 4, nhb, hb)
    g_row = jnp.transpose(g4, (1, 2, 3, 0))
    g_col = jnp.pad(jnp.transpose(g4, (1, 2, 0, 3)), ((0, 0), (0, 0), (0, 0), (0, LANES - hb)))

    cidx = lambda d, c: c + d * (nc - 1 - 2 * c)
    tile = pl.BlockSpec((ch, hb * dk), lambda d, h, c, r: (cidx(d, c), h))
    colspec = lambda off: pl.BlockSpec((1, 1, ch, LANES), lambda d, h, c, r: (2 * d + off, h, cidx(d, c), 0))
    rowspec = lambda off: pl.BlockSpec((1, 1, hb, ch), lambda d, h, c, r: (2 * d + off, h, 0, cidx(d, c)))
    kern = functools.partial(_mlstm_kernel, hb=hb, dk=dk)
    return pl.pallas_call(
        kern,
        name="mlstm_core",
        grid_spec=pltpu.PrefetchScalarGridSpec(
            num_scalar_prefetch=1,
            grid=(2, nhb, nc),
            in_specs=[tile, tile,
                      pl.BlockSpec((hb * dk, ch), lambda d, h, c, r: (h, cidx(d, c))),
                      tile, colspec(0), colspec(1), rowspec(0), rowspec(1)],
            out_specs=pl.BlockSpec((1, ch, hb * dk), lambda d, h, c, r: (d, cidx(d, c), h)),
            scratch_shapes=[pltpu.VMEM((hb, dk, dk), F32), pltpu.VMEM((hb, 1, dk), F32),
                            pltpu.VMEM((hb, 8, LANES), F32)]),
        out_shape=jax.ShapeDtypeStruct((2, t, e), F32),
        compiler_params=_params(("parallel", "parallel", "arbitrary")),
    )(reset, q, k, kt, v, g_col, g_col, g_row, g_row)


def _mlstm_post_kernel(hf_ref, hb_ref, o_ref, z_ref, xc_ref, hn_ref, sk_ref, out_ref):
    hs = hf_ref[0] + hb_ref[0]
    hs = hs * lax.rsqrt(jnp.mean(hs * hs, axis=-1, keepdims=True) + EPS)
    hs = hs * hn_ref[...]
    hs = _sigmoid(o_ref[...]) * hs + sk_ref[...] * xc_ref[...]
    out_ref[...] = (hs * _silu(z_ref[...])).astype(out_ref.dtype)


def mlstm_post(hdir, proj, xc, head_norm, skip, e, nh):
    t = xc.shape[0]
    dk = e // nh
    tm = _tile(t, 1024, 8)
    hspec = lambda d: pl.BlockSpec((1, tm, dk), lambda i, h: (d, i, h))
    pspec = lambda off: pl.BlockSpec((tm, dk), lambda i, h: (i, h + off * nh))
    vec = pl.BlockSpec((1, dk), lambda i, h: (0, h))
    return pl.pallas_call(
        _mlstm_post_kernel,
        name="mlstm_post",
        grid=(t // tm, nh),
        in_specs=[hspec(0), hspec(1), pspec(2), pspec(1), pspec(0), vec, vec],
        out_specs=pl.BlockSpec((tm, dk), lambda i, h: (i, h)),
        out_shape=jax.ShapeDtypeStruct((t, e), BF16),
        compiler_params=_params(("parallel", "parallel")),
    )(hdir, hdir, proj, proj, xc, head_norm.reshape(1, e).astype(F32), skip.reshape(1, e).astype(F32))


def mlstm_layer(x, seqs, mix_g, in_w, gate_b, conv_w, conv_b, wq, wk, wv, skip, head_norm, out_w):
    e = out_w.shape[0]
    nh = gate_b.shape[0] // 4
    h = rmsnorm(x, mix_g, BF16)
    proj = matmul(h, in_w[:, :3 * e].astype(BF16), out_dtype=F32)
    gw = jnp.pad(in_w[:, 3 * e:], ((0, 0), (0, LANES - 4 * nh))).astype(BF16)
    gb = jnp.pad(gate_b, (0, LANES - 4 * nh)).reshape(1, LANES).astype(F32)
    gates = matmul(h, gw, out_dtype=F32, mode="bias", extra=(gb,), extra_specs=(_row_spec,))
    xc, q, k, kt, v = conv_qkv(proj, seqs, conv_w, conv_b, wq, wk, wv, e, e // nh)
    hdir = mlstm_core(q, k, kt, v, gates, seqs, nh)
    mixed = mlstm_post(hdir, proj, xc, head_norm, skip, e, nh)
    return matmul(mixed, out_w.astype(BF16), out_dtype=F32, mode="residual",
                  extra=(x,), extra_specs=(_tile_spec(),), tk=2048)


def _xattn_kernel(seq_ref, q_ref, k_ref, v_ref, o_ref, *, scale):
    del seq_ref
    s = lax.dot_general(q_ref[...], k_ref[0], (((1,), (1,)), ((), ())), preferred_element_type=F32) * scale
    s = s - jnp.max(s, axis=-1, keepdims=True)
    p = jnp.exp(s)
    p = p / jnp.sum(p, axis=-1, keepdims=True)
    o_ref[...] = jnp.dot(p.astype(BF16), v_ref[0], preferred_element_type=F32).astype(o_ref.dtype)


def xattn_core(q, kmem, vmem, seqs):
    t, d = q.shape
    hd = d // XA_HEADS
    lmin = functools.reduce(math.gcd, [l for _, l in seqs])
    tm = _tile(lmin, 512, 8)
    seq_of = jnp.array([max(j for j, (s, _) in enumerate(seqs) if s <= i * tm) for i in range(t // tm)],
                       jnp.int32)
    m = kmem.shape[1]
    kern = functools.partial(_xattn_kernel, scale=float(hd) ** -0.5)
    mem_spec = pl.BlockSpec((1, m, hd), lambda i, h, sq: (sq[i], 0, h))
    return pl.pallas_call(
        kern,
        name="xattn_core",
        grid_spec=pltpu.PrefetchScalarGridSpec(
            num_scalar_prefetch=1,
            grid=(t // tm, XA_HEADS),
            in_specs=[pl.BlockSpec((tm, hd), lambda i, h, sq: (i, h)), mem_spec, mem_spec],
            out_specs=pl.BlockSpec((tm, hd), lambda i, h, sq: (i, h))),
        out_shape=jax.ShapeDtypeStruct((t, d), BF16),
        compiler_params=_params(("parallel", "parallel")),
    )(seq_of, q, kmem, vmem)


def xattn_layer(x, mem, seqs, xa_g, mem_g, wq, wk, wv, wo):
    ns, m, d = mem.shape
    hq = rmsnorm(x, xa_g, BF16)
    mem_n = rmsnorm(mem.reshape(ns * m, d), mem_g, BF16)
    q = matmul(hq, wq.astype(BF16), out_dtype=BF16)
    kmem = matmul(mem_n, wk.astype(BF16), out_dtype=BF16).reshape(ns, m, d)
    vmem = matmul(mem_n, wv.astype(BF16), out_dtype=BF16).reshape(ns, m, d)
    o = xattn_core(q, kmem, vmem, seqs)
    return matmul(o, wo.astype(BF16), out_dtype=F32, mode="residual", extra=(x,), extra_specs=(_tile_spec(),))


def kernel(x_prompt, x_sample, mem_prompt, mem_sample, mix_norm, a_in_w, s5_a_re, s5_a_im, s5_log_step,
           s5_b_re, s5_b_im, s5_c_re, s5_c_im, s5_d, a_glu_w, a_glu_b, a_out_w, b_in_w, b_gate_bias,
           b_conv_w, b_conv_b, b_wq, b_wk, b_wv, b_skip, b_head_norm, b_out_w, xa_norm, mem_norm,
           xa_wq, xa_wk, xa_wv, xa_wo, final_norm):
    bp, lp, d = x_prompt.shape
    bs, ls, _ = x_sample.shape
    depth = mix_norm.shape[0]
    seqs = [(i * lp, lp) for i in range(bp)] + [(bp * lp + i * ls, ls) for i in range(bs)]
    x = jnp.concatenate([x_prompt.reshape(bp * lp, d), x_sample.reshape(bs * ls, d)], axis=0)
    mem = jnp.concatenate([mem_prompt, mem_sample], axis=0)
    s5_ops = jax.vmap(_s5_operators)(s5_a_re, s5_a_im, s5_log_step, s5_b_re, s5_b_im, s5_c_re, s5_c_im)
    for i in range(depth):
        j = i // 2
        if i % 2 == 0:
            x = s5_layer(x, seqs, mix_norm[i], a_in_w[j], s5_ops, j, s5_d[j], a_glu_w[j], a_glu_b[j], a_out_w[j])
        else:
            x = mlstm_layer(x, seqs, mix_norm[i], b_in_w[j], b_gate_bias[j], b_conv_w[j], b_conv_b[j],
                            b_wq[j], b_wk[j], b_wv[j], b_skip[j], b_head_norm[j], b_out_w[j])
        x = xattn_layer(x, mem, seqs, xa_norm[i], mem_norm[i], xa_wq[i], xa_wk[i], xa_wv[i], xa_wo[i])
    y = rmsnorm(x, final_norm, F32)
    return (y[:bp * lp].reshape(bp, lp, d), y[bp * lp:].reshape(bs, ls, d))
```
